```python
import jax, jax.numpy as jnp
from jax import lax
import numpy as np

D_MODEL = 1024
BATCH = 4
SEQ = 8192
DEPTH = 1

CTX_LEN = 256
GRID_W = 64
MIX_W = D_MODEL
CONV_W = MIX_W // 2
CONV_K = 31
GLA_HEADS = 4
GLA_DV = MIX_W - CONV_W
GLA_HEAD_DV = GLA_DV // GLA_HEADS
GLA_DK = GLA_DV // 2
GLA_HEAD_DK = GLA_DK // GLA_HEADS
GATE_RANK = 16
GATE_NORM = 16.0
CHUNK = 64
FFN_HIDDEN = ((8 * D_MODEL // 3 + 127) // 128) * 128
FFN_K = 3
N_MOD = 6
EPS = 1e-6
IN_SPLITS = (CONV_W, CONV_W, GLA_DK, GLA_DK, GLA_DV, GLA_DV, GATE_RANK, GATE_RANK)
D_IN = 2 * CONV_W + 2 * GLA_DK + 2 * GLA_DV + 2 * GATE_RANK

kernel_name = "hybrid_conformer_gla_dit_layer"


def _rmsnorm(x, g):
    xf = x.astype(jnp.float32)
    y = xf * lax.rsqrt(jnp.mean(xf * xf, axis=-1, keepdims=True) + EPS)
    return (y * g.astype(jnp.float32)).astype(x.dtype)


def _layernorm(x, g, b):
    xf = x.astype(jnp.float32)
    mu = jnp.mean(xf, axis=-1, keepdims=True)
    var = jnp.mean(jnp.square(xf - mu), axis=-1, keepdims=True)
    y = (xf - mu) * lax.rsqrt(var + EPS) * g.astype(jnp.float32) + b.astype(jnp.float32)
    return y.astype(x.dtype)


def _adaln(cvec, w_mod, b_mod):
    m = jax.nn.silu(cvec) @ w_mod + b_mod
    return jnp.split(m[:, None, :], N_MOD, axis=-1)


def _modulate(h, shift, scale):
    return h * (1.0 + scale) + shift


def _dwconv_seq(x, w, b):
    k, ch = w.shape
    y = lax.conv_general_dilated(x, w.reshape(k, 1, ch).astype(x.dtype), (1,), [(k // 2, k // 2)],
                                 dimension_numbers=('NWC', 'WIO', 'NWC'), feature_group_count=ch)
    return y + b


def _dwconv_grid(xg, w, axis):
    k, ch = w.shape
    if axis == 1:
        kern, pad = w.reshape(k, 1, 1, ch), [(k // 2, k // 2), (0, 0)]
    else:
        kern, pad = w.reshape(1, k, 1, ch), [(0, 0), (k // 2, k // 2)]
    return lax.conv_general_dilated(xg, kern.astype(xg.dtype), (1, 1), pad,
                                    dimension_numbers=('NHWC', 'HWIO', 'NHWC'), feature_group_count=ch)


def _split_proj(h, p):
    idx = np.cumsum(IN_SPLITS)[:-1].tolist()
    return jnp.split(h @ p['w_in'], idx, axis=-1)


def _conv_module(u, gate, p, grid):
    glu = u * jax.nn.sigmoid(gate)
    if grid:
        bsz, length, _ = glu.shape
        rows = length // GRID_W
        g = glu.reshape(bsz, rows, GRID_W, CONV_W)
        half = CONV_W // 2
        y = jnp.concatenate([_dwconv_grid(g[..., :half], p['conv_dw'][:, :half], 2),
                             _dwconv_grid(g[..., half:], p['conv_dw'][:, half:], 1)], axis=-1)
        y = y.reshape(bsz, length, CONV_W) + p['conv_b']
    else:
        y = _dwconv_seq(glu, p['conv_dw'], p['conv_b'])
    return jax.nn.silu(_layernorm(y, p['conv_ln_g'], p['conv_ln_b']))


def _gla_inputs(q, k, v, zf, zb, p):
    bsz, length, _ = q.shape
    f32 = jnp.float32
    heads = lambda t, d: t.astype(f32).reshape(bsz, length, GLA_HEADS, d)
    q = heads(q, GLA_HEAD_DK) * (GLA_HEAD_DK ** -0.5)
    k = heads(k, GLA_HEAD_DK)
    v = heads(v, GLA_HEAD_DV)
    gf = heads(jax.nn.log_sigmoid((zf @ p['w_gf'] + p['b_gf']).astype(f32)) / GATE_NORM, GLA_HEAD_DK)
    gb = heads(jax.nn.log_sigmoid((zb @ p['w_gb'] + p['b_gb']).astype(f32)) / GATE_NORM, GLA_HEAD_DK)
    return q, k, v, gf, gb


def _gla_chunked(q, k, v, g, s0):
    bsz, length, nh, dk = q.shape
    dv = v.shape[-1]
    n = length // CHUNK
    q, k, g = (t.reshape(bsz, n, CHUNK, nh, dk) for t in (q, k, g))
    v = v.reshape(bsz, n, CHUNK, nh, dv)
    b = jnp.cumsum(g, axis=2)
    b_last = b[:, :, -1]
    q_e = q * jnp.exp(b)
    k_e = k * jnp.exp(-b)
    k_tail = k * jnp.exp(b_last[:, :, None] - b)
    mask = jnp.tril(jnp.ones((CHUNK, CHUNK), dtype=bool))
    scores = jnp.where(mask, jnp.einsum('bnthd,bnshd->bnhts', q_e, k_e), 0.0)
    o_intra = jnp.einsum('bnhts,bnshv->bnthv', scores, v)
    kv = jnp.einsum('bnshd,bnshv->nbhdv', k_tail, v)
    decay = jnp.moveaxis(jnp.exp(b_last), 1, 0)

    def step(state, inp):
        kv_c, dec_c = inp
        return dec_c[..., None] * state + kv_c, state

    s_fin, s_prev = lax.scan(step, s0, (kv, decay))
    o_inter = jnp.einsum('bnthd,nbhdv->bnthv', q_e, s_prev)
    return (o_intra + o_inter).reshape(bsz, length, nh, dv), s_fin


def _gla_final_state(k, v, g):
    b = jnp.cumsum(g, axis=1)
    k_tail = k * jnp.exp(b[:, -1:] - b)
    return jnp.einsum('blhd,blhv->bhdv', k_tail, v)


def _rev(t):
    return jnp.flip(t, axis=1)


def _token_mixer(h, p, s_f0, s_b0, grid):
    bsz, length, _ = h.shape
    cu, cg, q, k, v, og, zf, zb = _split_proj(h, p)
    y_conv = _conv_module(cu, cg, p, grid)
    q, k, v, gf, gb = _gla_inputs(q, k, v, zf, zb, p)
    o_f, s_f = _gla_chunked(q, k, v, gf, s_f0)
    o_b, s_b = _gla_chunked(_rev(q), _rev(k), _rev(v), _rev(gb), s_b0)
    o = _rmsnorm(o_f + _rev(o_b), p['gla_norm_g'].reshape(GLA_HEADS, GLA_HEAD_DV))
    o = o.reshape(bsz, length, GLA_DV).astype(h.dtype) * jax.nn.silu(og)
    y = jnp.concatenate([y_conv, o], axis=-1) @ p['w_out']
    return y, s_f, s_b


def _context_states(h_ctx, p):
    _, _, q, k, v, _, zf, zb = _split_proj(h_ctx, p)
    _, k, v, gf, gb = _gla_inputs(q, k, v, zf, zb, p)
    return _gla_final_state(k, v, gf), _gla_final_state(_rev(k), _rev(v), _rev(gb))


def _conv_ffn(h, p):
    a, val = jnp.split(h @ p['w_up'], 2, axis=-1)
    a = _dwconv_seq(a, p['ffn_dw'], p['ffn_dw_b'])
    return (jax.nn.silu(a) * val) @ p['w_down']


def setup_inputs(seed: int = 0) -> dict:
    key = jax.random.key(seed)
    ks = jax.random.split(key, 32)
    f32 = jnp.float32
    nrm = lambda k, shape, s: jax.random.normal(k, shape, f32) * s
    L = DEPTH
    return {
        'x': nrm(ks[0], (BATCH, SEQ, D_MODEL), 1.0),
        'c': nrm(ks[1], (BATCH, D_MODEL), 1.0),
        'ctx': nrm(ks[2], (BATCH, CTX_LEN, D_MODEL), 1.0),
        'c_ctx': nrm(ks[3], (D_MODEL,), 1.0),
        'w_mod': nrm(ks[4], (L, D_MODEL, N_MOD * D_MODEL), 0.5 * D_MODEL ** -0.5),
        'b_mod': nrm(ks[5], (L, N_MOD * D_MODEL), 0.02),
        'norm1_g': 1.0 + nrm(ks[6], (L, D_MODEL), 0.02),
        'w_in': nrm(ks[7], (L, D_MODEL, D_IN), D_MODEL ** -0.5),
        'conv_dw': nrm(ks[8], (L, CONV_K, CONV_W), CONV_K ** -0.5),
        'conv_b': nrm(ks[9], (L, CONV_W), 0.02),
        'conv_ln_g': 1.0 + nrm(ks[10], (L, CONV_W), 0.02),
        'conv_ln_b': nrm(ks[11], (L, CONV_W), 0.02),
        'w_gf': nrm(ks[12], (L, GATE_RANK, GLA_DK), GATE_RANK ** -0.5),
        'b_gf': nrm(ks[13], (L, GLA_DK), 0.1),
        'w_gb': nrm(ks[14], (L, GATE_RANK, GLA_DK), GATE_RANK ** -0.5),
        'b_gb': nrm(ks[15], (L, GLA_DK), 0.1),
        'gla_norm_g': 1.0 + nrm(ks[16], (L, GLA_DV), 0.02),
        'w_out': nrm(ks[17], (L, MIX_W, D_MODEL), MIX_W ** -0.5),
        'norm2_g': 1.0 + nrm(ks[18], (L, D_MODEL), 0.02),
        'w_up': nrm(ks[19], (L, D_MODEL, 2 * FFN_HIDDEN), D_MODEL ** -0.5),
        'ffn_dw': nrm(ks[20], (L, FFN_K, FFN_HIDDEN), FFN_K ** -0.5),
        'ffn_dw_b': nrm(ks[21], (L, FFN_HIDDEN), 0.02),
        'w_down': nrm(ks[22], (L, FFN_HIDDEN, D_MODEL), FFN_HIDDEN ** -0.5),
        'final_g': 1.0 + nrm(ks[23], (D_MODEL,), 0.02),
    }


def reference(x, c, ctx, c_ctx, w_mod, b_mod, norm1_g, w_in, conv_dw, conv_b, conv_ln_g, conv_ln_b,
              w_gf, b_gf, w_gb, b_gb, gla_norm_g, w_out, norm2_g, w_up, ffn_dw, ffn_dw_b, w_down, final_g):
    bsz = x.shape[0]
    x_ctx = ctx
    for l in range(DEPTH):
        p = {'w_in': w_in[l], 'conv_dw': conv_dw[l], 'conv_b': conv_b[l], 'conv_ln_g': conv_ln_g[l],
             'conv_ln_b': conv_ln_b[l], 'w_gf': w_gf[l], 'b_gf': b_gf[l], 'w_gb': w_gb[l], 'b_gb': b_gb[l],
             'gla_norm_g': gla_norm_g[l], 'w_out': w_out[l], 'w_up': w_up[l], 'ffn_dw': ffn_dw[l],
             'ffn_dw_b': ffn_dw_b[l], 'w_down': w_down[l]}
        sh1, sc1, g1, sh2, sc2, g2 = _adaln(c, w_mod[l], b_mod[l])
        sh1c, sc1c, g1c, sh2c, sc2c, g2c = _adaln(c_ctx[None, :], w_mod[l], b_mod[l])
        h_ctx = _modulate(_rmsnorm(x_ctx, norm1_g[l]), sh1c, sc1c)
        if l == DEPTH - 1:
            s_f, s_b = _context_states(h_ctx, p)
        else:
            zeros = jnp.zeros((bsz, GLA_HEADS, GLA_HEAD_DK, GLA_HEAD_DV), jnp.float32)
            y_ctx, s_f, s_b = _token_mixer(h_ctx, p, zeros, zeros, grid=False)
            x_ctx = x_ctx + g1c * y_ctx
            x_ctx = x_ctx + g2c * _conv_ffn(_modulate(_rmsnorm(x_ctx, norm2_g[l]), sh2c, sc2c), p)
        h = _modulate(_rmsnorm(x, norm1_g[l]), sh1, sc1)
        y, _, _ = _token_mixer(h, p, s_f, s_b, grid=True)
        x = x + g1 * y
        x = x + g2 * _conv_ffn(_modulate(_rmsnorm(x, norm2_g[l]), sh2, sc2), p)
    return _rmsnorm(x, final_g)
```

```python
import functools

import jax
import jax.numpy as jnp
from jax import lax
from jax.experimental import pallas as pl
from jax.experimental.pallas import tpu as pltpu

F32 = jnp.float32
BF16 = jnp.bfloat16

D_MODEL = 1024
GRID_W = 64
CONV_W = 512
CONV_HALF = CONV_W // 2
CONV_K = 31
CONV_PAD = CONV_K // 2
GLA_HEADS = 4
GLA_DV = 512
GLA_HEAD_DV = 128
GLA_DK = 256
GLA_HEAD_DK = 64
GATE_RANK = 16
GATE_NORM = 16.0
CHUNK = 64
FFN_HIDDEN = 2816
N_MOD = 6
EPS = 1e-6

LANES = 128
SUBLANES = 8
V7X_VMEM_BYTES = 64 * 1024 * 1024
VMEM_LIMIT = V7X_VMEM_BYTES - 8 * 1024 * 1024

Z_PAD = LANES
D_IN_PAD = 2 * CONV_W + 2 * GLA_DK + 2 * GLA_DV + Z_PAD
OFF_CU, OFF_CG = 0, CONV_W
OFF_Q = 2 * CONV_W
OFF_K = OFF_Q + GLA_DK
OFF_V = OFF_K + GLA_DK
OFF_OG = OFF_V + GLA_DV
OFF_Z = OFF_OG + GLA_DV

GLA_BLOCK = 256
CONV_ROWS = 16
FFN_TILE = 512
FFN_CHUNK = 256
HALO = SUBLANES


def _sigmoid(x):
    return 1.0 / (1.0 + jnp.exp(-x))


def _split_bf16(x):
    hi = x.astype(BF16)
    lo = (x - hi.astype(F32)).astype(BF16)
    return hi, lo


def _dot(a, b):
    return jnp.dot(a, b, preferred_element_type=F32)


def _mod_kernel(c_ref, w_ref, b_ref, o_ref):
    c = c_ref[...]
    s = c * _sigmoid(c)
    s_hi, s_lo = _split_bf16(s)
    w_hi, w_lo = _split_bf16(w_ref[...])
    o_ref[...] = _dot(s_hi, w_hi) + _dot(s_lo, w_hi) + _dot(s_hi, w_lo) + b_ref[...]


def _modulation(cvec, w_mod, b_mod):
    rows, d = cvec.shape
    n = w_mod.shape[1]
    bn = D_MODEL
    return pl.pallas_call(
        _mod_kernel,
        grid=(n // bn,),
        in_specs=[pl.BlockSpec((rows, d), lambda j: (0, 0)),
                  pl.BlockSpec((d, bn), lambda j: (0, j)),
                  pl.BlockSpec((1, bn), lambda j: (0, j))],
        out_specs=pl.BlockSpec((rows, bn), lambda j: (0, j)),
        out_shape=jax.ShapeDtypeStruct((rows, n), F32),
        compiler_params=pltpu.CompilerParams(dimension_semantics=("arbitrary",),
                                             vmem_limit_bytes=VMEM_LIMIT),
        name="modulation",
    )(cvec, w_mod, b_mod)


def _in_kernel(x_ref, mod_ref, g_ref, w_ref, glu_ref, q_ref, k_ref, v_ref, og_ref, z_ref):
    x = x_ref[0]
    ms = jnp.mean(x * x, axis=-1, keepdims=True)
    y = x * lax.rsqrt(ms + EPS) * g_ref[...]
    m = mod_ref[0]
    shift = m[:, 0:D_MODEL]
    scale = m[:, D_MODEL:2 * D_MODEL]
    h = (y * (1.0 + scale) + shift).astype(BF16)

    def proj(lo, width):
        return _dot(h, w_ref[:, lo:lo + width])

    glu_ref[0] = proj(OFF_CU, CONV_W) * _sigmoid(proj(OFF_CG, CONV_W))
    q_ref[0] = proj(OFF_Q, GLA_DK)
    k_ref[0] = proj(OFF_K, GLA_DK)
    v_ref[0] = proj(OFF_V, GLA_DV)
    og_ref[0] = proj(OFF_OG, GLA_DV)
    z_ref[0] = proj(OFF_Z, Z_PAD)


def _input_proj(x, mod, norm_g, w_in_p, tm):
    bsz, length, d = x.shape
    widths = (CONV_W, GLA_DK, GLA_DK, GLA_DV, GLA_DV, Z_PAD)
    return pl.pallas_call(
        _in_kernel,
        grid=(bsz, length // tm),
        in_specs=[pl.BlockSpec((1, tm, d), lambda b, i: (b, i, 0)),
                  pl.BlockSpec((1, 1, N_MOD * d), lambda b, i: (b, 0, 0)),
                  pl.BlockSpec((1, d), lambda b, i: (0, 0)),
                  pl.BlockSpec((d, D_IN_PAD), lambda b, i: (0, 0))],
        out_specs=[pl.BlockSpec((1, tm, w), lambda b, i: (b, i, 0)) for w in widths],
        out_shape=[jax.ShapeDtypeStruct((bsz, length, w), F32) for w in widths],
        compiler_params=pltpu.CompilerParams(dimension_semantics=("arbitrary", "arbitrary"),
                                             vmem_limit_bytes=VMEM_LIMIT),
        name="input_proj",
    )(x, mod, norm_g, w_in_p)


def _gla_chunk(qc, kc, vc, zc, wg, bg, s_ref, reverse):
    row = lax.broadcasted_iota(jnp.int32, (CHUNK, CHUNK), 0)
    col = lax.broadcasted_iota(jnp.int32, (CHUNK, CHUNK), 1)
    keep = (col >= row) if reverse else (col <= row)
    tri = keep.astype(BF16)
    row2 = lax.broadcasted_iota(jnp.int32, (CHUNK, LANES), 0)
    col2 = lax.broadcasted_iota(jnp.int32, (CHUNK, LANES), 1) & (CHUNK - 1)
    keep_pair = (col2 >= row2) if reverse else (col2 <= row2)

    pre = _dot(zc[:, 0:2 * GATE_RANK].astype(BF16), wg) + bg
    g = (jnp.minimum(pre, 0.0) - jnp.log1p(jnp.exp(-jnp.abs(pre)))) * (1.0 / GATE_NORM)
    g_hi, g_lo = _split_bf16(g)
    b = _dot(tri, g_hi) + _dot(tri, g_lo)
    b_last = b[0:1] if reverse else b[CHUNK - 1:CHUNK]
    q_e = qc * (GLA_HEAD_DK ** -0.5) * jnp.exp(b)
    k_e = kc * jnp.exp(-b)
    k_t = kc * jnp.exp(b_last - b)
    decay = jnp.broadcast_to(jnp.exp(b_last), (LANES, GLA_DK)).T

    lane = lax.broadcasted_iota(jnp.int32, (CHUNK, LANES), 1)
    first = lane < GLA_HEAD_DK
    outs = []
    for p in range(GLA_HEADS // 2):
        lo = p * LANES
        qg = q_e[:, lo:lo + LANES].astype(BF16)
        kg = k_e[:, lo:lo + LANES]
        k_blk = jnp.concatenate([jnp.where(first, kg, 0.0), jnp.where(first, 0.0, kg)],
                                axis=0).astype(BF16)
        scores = lax.dot_general(qg, k_blk, (((1,), (1,)), ((), ())),
                                 preferred_element_type=F32)
        a = jnp.where(keep_pair, scores, 0.0).astype(BF16)
        vp = vc[:, 2 * lo:2 * lo + 2 * LANES]
        zero = jnp.zeros((CHUNK, LANES), F32)
        v_blk = jnp.concatenate(
            [jnp.concatenate([vp[:, :LANES], zero], axis=1),
             jnp.concatenate([zero, vp[:, LANES:]], axis=1)], axis=0).astype(BF16)
        s_blk = s_ref[p]
        outs.append(_dot(qg, s_blk.astype(BF16)) + _dot(a, v_blk))
        kv = lax.dot_general(k_t[:, lo:lo + LANES].astype(BF16), vp.astype(BF16),
                             (((0,), (0,)), ((), ())), preferred_element_type=F32)
        dp = decay[lo:lo + LANES]
        h0 = slice(0, GLA_HEAD_DK)
        h1 = slice(GLA_HEAD_DK, 2 * GLA_HEAD_DK)
        c0 = slice(0, LANES)
        c1 = slice(LANES, 2 * LANES)
        s_ref[p, h0, c0] = dp[h0] * s_blk[h0, c0] + kv[h0, c0]
        s_ref[p, h1, c1] = dp[h1] * s_blk[h1, c1] + kv[h1, c1]
    return jnp.concatenate(outs, axis=1)


def _gla_kernel(qf_ref, kf_ref, vf_ref, zf_ref, qb_ref, kb_ref, vb_ref, zb_ref,
                wgf_ref, bgf_ref, wgb_ref, bgb_ref, sf0_ref, sb0_ref,
                of_ref, ob_ref, sf_out_ref, sb_out_ref, sf_ref, sb_ref):
    i = pl.program_id(1)
    h0 = slice(0, GLA_HEAD_DK)
    h1 = slice(GLA_HEAD_DK, 2 * GLA_HEAD_DK)
    c0 = slice(0, LANES)
    c1 = slice(LANES, 2 * LANES)

    @pl.when(i == 0)
    def _():
        for s_ref, s0_ref in ((sf_ref, sf0_ref), (sb_ref, sb0_ref)):
            s_ref[...] = jnp.zeros(s_ref.shape, F32)
            for p in range(GLA_HEADS // 2):
                s_ref[p, h0, c0] = s0_ref[0, 2 * p]
                s_ref[p, h1, c1] = s0_ref[0, 2 * p + 1]

    n_chunks = GLA_BLOCK // CHUNK
    for c in range(n_chunks):
        rows = slice(c * CHUNK, (c + 1) * CHUNK)
        of_ref[0, rows, :] = _gla_chunk(qf_ref[0, rows, :], kf_ref[0, rows, :], vf_ref[0, rows, :],
                                        zf_ref[0, rows, :], wgf_ref[...], bgf_ref[...], sf_ref, False)
        rc = n_chunks - 1 - c
        rows = slice(rc * CHUNK, (rc + 1) * CHUNK)
        ob_ref[0, rows, :] = _gla_chunk(qb_ref[0, rows, :], kb_ref[0, rows, :], vb_ref[0, rows, :],
                                        zb_ref[0, rows, :], wgb_ref[...], bgb_ref[...], sb_ref, True)

    @pl.when(i == pl.num_programs(1) - 1)
    def _():
        for s_ref, out_ref in ((sf_ref, sf_out_ref), (sb_ref, sb_out_ref)):
            for p in range(GLA_HEADS // 2):
                out_ref[0, 2 * p] = s_ref[p, h0, c0]
                out_ref[0, 2 * p + 1] = s_ref[p, h1, c1]


def _gla(q, k, v, z, wgf, bgf, wgb, bgb, sf0, sb0):
    bsz, length, _ = q.shape
    tb = min(GLA_BLOCK, length)
    assert tb == GLA_BLOCK and length % tb == 0
    nb = length // tb

    def fwd(w):
        return pl.BlockSpec((1, tb, w), lambda b, i: (b, i, 0))

    def bwd(w):
        return pl.BlockSpec((1, tb, w), lambda b, i: (b, nb - 1 - i, 0))

    def const2(shape):
        return pl.BlockSpec(shape, lambda b, i: (0, 0))

    state = pl.BlockSpec((1, GLA_HEADS, GLA_HEAD_DK, GLA_HEAD_DV), lambda b, i: (b, 0, 0, 0))
    state_shape = jax.ShapeDtypeStruct((bsz, GLA_HEADS, GLA_HEAD_DK, GLA_HEAD_DV), F32)
    o_shape = jax.ShapeDtypeStruct((bsz, length, GLA_DV), F32)
    return pl.pallas_call(
        _gla_kernel,
        grid=(bsz, nb),
        in_specs=[fwd(GLA_DK), fwd(GLA_DK), fwd(GLA_DV), fwd(Z_PAD),
                  bwd(GLA_DK), bwd(GLA_DK), bwd(GLA_DV), bwd(Z_PAD),
                  const2((2 * GATE_RANK, GLA_DK)), const2((1, GLA_DK)),
                  const2((2 * GATE_RANK, GLA_DK)), const2((1, GLA_DK)),
                  state, state],
        out_specs=[fwd(GLA_DV), bwd(GLA_DV), state, state],
        out_shape=[o_shape, o_shape, state_shape, state_shape],
        scratch_shapes=[pltpu.VMEM((GLA_HEADS // 2, 2 * GLA_HEAD_DK, 2 * GLA_HEAD_DV), F32),
                        pltpu.VMEM((GLA_HEADS // 2, 2 * GLA_HEAD_DK, 2 * GLA_HEAD_DV), F32)],
        compiler_params=pltpu.CompilerParams(dimension_semantics=("arbitrary", "arbitrary"),
                                             vmem_limit_bytes=VMEM_LIMIT),
        name="gla",
    )(q, k, v, z, q, k, v, z, wgf, bgf, wgb, bgb, sf0, sb0)


def _conv_kernel(grow_ref, gcol_ref, wrow_ref, wcol_ref, cb_ref, lg_ref, lb_ref, y_ref,
                 rowpad_ref, colpad_ref):
    t = pl.program_id(1)
    n_rows = gcol_ref.shape[1]
    pad_rows = CONV_ROWS
    pad_w = 2 * SUBLANES

    @pl.when(t == 0)
    def _():
        rowpad_ref[...] = jnp.zeros(rowpad_ref.shape, F32)
        zeros = jnp.zeros((pad_rows, GRID_W, CONV_HALF), F32)
        colpad_ref[0:pad_rows] = zeros
        colpad_ref[pad_rows + n_rows:pad_rows + n_rows + pad_rows] = zeros

        def copy(j, carry):
            colpad_ref[pl.ds(pad_rows + j * CONV_ROWS, CONV_ROWS)] = gcol_ref[0, pl.ds(j * CONV_ROWS, CONV_ROWS)]
            return carry
        lax.fori_loop(0, n_rows // CONV_ROWS, copy, 0)

    rowpad_ref[:, pad_w:pad_w + GRID_W, :] = grow_ref[0]

    def row_body(r, carry):
        pieces = []
        for lg in range(CONV_HALF // LANES):
            lanes = slice(lg * LANES, (lg + 1) * LANES)
            acc = jnp.zeros((GRID_W, LANES), F32)
            for kk in range(CONV_K):
                start = pad_w + kk - CONV_PAD
                acc = acc + wrow_ref[kk:kk + 1, lanes] * rowpad_ref[r, start:start + GRID_W, lanes]
            pieces.append(acc)
        for lg in range(CONV_HALF // LANES):
            lanes = slice(lg * LANES, (lg + 1) * LANES)
            acc = jnp.zeros((GRID_W, LANES), F32)
            for kk in range(CONV_K):
                src = t * CONV_ROWS + r + (pad_rows + kk - CONV_PAD)
                acc = acc + wcol_ref[kk:kk + 1, lanes] * colpad_ref[src, :, lanes]
            pieces.append(acc)
        y = jnp.concatenate(pieces, axis=1) + cb_ref[...]
        mu = jnp.mean(y, axis=-1, keepdims=True)
        yc = y - mu
        var = jnp.mean(yc * yc, axis=-1, keepdims=True)
        yn = yc * lax.rsqrt(var + EPS) * lg_ref[...] + lb_ref[...]
        y_ref[0, r] = yn * _sigmoid(yn)
        return carry

    lax.fori_loop(0, CONV_ROWS, row_body, 0)


def _conv_module(glu4, w_row, w_col, conv_b, ln_g, ln_b):
    bsz, n_rows, gw, _ = glu4.shape
    return pl.pallas_call(
        _conv_kernel,
        grid=(bsz, n_rows // CONV_ROWS),
        in_specs=[pl.BlockSpec((1, CONV_ROWS, gw, CONV_HALF), lambda b, t: (b, t, 0, 0)),
                  pl.BlockSpec((1, n_rows, gw, CONV_HALF), lambda b, t: (b, 0, 0, 1)),
                  pl.BlockSpec((4 * SUBLANES, CONV_HALF), lambda b, t: (0, 0)),
                  pl.BlockSpec((4 * SUBLANES, CONV_HALF), lambda b, t: (0, 0)),
                  pl.BlockSpec((1, CONV_W), lambda b, t: (0, 0)),
                  pl.BlockSpec((1, CONV_W), lambda b, t: (0, 0)),
                  pl.BlockSpec((1, CONV_W), lambda b, t: (0, 0))],
        out_specs=pl.BlockSpec((1, CONV_ROWS, gw, CONV_W), lambda b, t: (b, t, 0, 0)),
        out_shape=jax.ShapeDtypeStruct((bsz, n_rows, gw, CONV_W), F32),
        scratch_shapes=[pltpu.VMEM((CONV_ROWS, gw + 4 * SUBLANES, CONV_HALF), F32),
                        pltpu.VMEM((n_rows + 2 * CONV_ROWS, gw, CONV_HALF), F32)],
        compiler_params=pltpu.CompilerParams(dimension_semantics=("arbitrary", "arbitrary"),
                                             vmem_limit_bytes=VMEM_LIMIT),
        name="conv_module",
    )(glu4, glu4, w_row, w_col, conv_b, ln_g, ln_b)


def _ffn_kernel(x_p, x_m, x_n, yc_p, yc_m, yc_n, of_p, of_m, of_n, ob_p, ob_m, ob_n,
                og_p, og_m, og_n, mod_ref, gn_ref, n2_ref, fg_ref, wout_ref, wup_ref,
                dw_ref, dwb_ref, wdown_ref, out_ref, a_ref):
    i = pl.program_id(1)
    last = pl.num_programs(1) - 1
    tm = x_m.shape[1]
    main = slice(HALO, HALO + tm)

    def ext(p, m, n):
        return jnp.concatenate([p[0], m[0], n[0]], axis=0)

    mod = mod_ref[0]
    gate1 = mod[:, 2 * D_MODEL:3 * D_MODEL]
    shift2 = mod[:, 3 * D_MODEL:4 * D_MODEL]
    scale2 = mod[:, 4 * D_MODEL:5 * D_MODEL]
    gate2 = mod[:, 5 * D_MODEL:6 * D_MODEL]

    o = ext(of_p, of_m, of_n) + ext(ob_p, ob_m, ob_n)
    og = ext(og_p, og_m, og_n)
    parts = [ext(yc_p, yc_m, yc_n).astype(BF16)]
    for h in range(GLA_HEADS):
        lanes = slice(h * GLA_HEAD_DV, (h + 1) * GLA_HEAD_DV)
        oh = o[:, lanes]
        ms = jnp.mean(oh * oh, axis=-1, keepdims=True)
        ogh = og[:, lanes]
        parts.append((oh * lax.rsqrt(ms + EPS) * gn_ref[:, lanes] * (ogh * _sigmoid(ogh))).astype(BF16))
    mix = jnp.concatenate(parts, axis=1)
    x1 = ext(x_p, x_m, x_n) + gate1 * _dot(mix, wout_ref[...])

    ms = jnp.mean(x1 * x1, axis=-1, keepdims=True)
    h2 = ((x1 * lax.rsqrt(ms + EPS) * n2_ref[...]) * (1.0 + scale2) + shift2).astype(BF16)

    row = lax.broadcasted_iota(jnp.int32, (tm, 1), 0)
    kill_prev = jnp.logical_and(row == 0, i == 0)
    kill_next = jnp.logical_and(row == tm - 1, i == last)
    acc = jnp.zeros((tm, D_MODEL), F32)
    for j in range(FFN_HIDDEN // FFN_CHUNK):
        cols = slice(j * FFN_CHUNK, (j + 1) * FFN_CHUNK)
        vcols = slice(FFN_HIDDEN + j * FFN_CHUNK, FFN_HIDDEN + (j + 1) * FFN_CHUNK)
        a_ref[...] = _dot(h2, wup_ref[:, cols])
        val = _dot(h2, wup_ref[:, vcols])[main]
        a_prev = jnp.where(kill_prev, 0.0, a_ref[HALO - 1:HALO - 1 + tm, :])
        a_next = jnp.where(kill_next, 0.0, a_ref[HALO + 1:HALO + 1 + tm, :])
        ac = (dw_ref[0:1, cols] * a_prev + dw_ref[1:2, cols] * a_ref[main, :]
              + dw_ref[2:3, cols] * a_next + dwb_ref[:, cols])
        f = (ac * _sigmoid(ac) * val).astype(BF16)
        acc = acc + _dot(f, wdown_ref[cols, :])

    x2 = x1[main] + gate2 * acc
    ms = jnp.mean(x2 * x2, axis=-1, keepdims=True)
    out_ref[0] = x2 * lax.rsqrt(ms + EPS) * fg_ref[...]


def _ffn(x, y_conv, o_f, o_b, og, mod, gn, n2, fg, w_out, w_up, dw, dwb, w_down):
    bsz, length, d = x.shape
    tm = FFN_TILE
    per = tm // HALO
    n_halo = length // HALO

    def trio(w):
        return [pl.BlockSpec((1, HALO, w), lambda b, i: (b, jnp.maximum(i * per - 1, 0), 0)),
                pl.BlockSpec((1, tm, w), lambda b, i: (b, i, 0)),
                pl.BlockSpec((1, HALO, w), lambda b, i: (b, jnp.minimum((i + 1) * per, n_halo - 1), 0))]

    def const2(shape):
        return pl.BlockSpec(shape, lambda b, i: (0, 0))

    in_specs = (trio(d) + trio(CONV_W) + trio(GLA_DV) + trio(GLA_DV) + trio(GLA_DV)
                + [pl.BlockSpec((1, 1, N_MOD * d), lambda b, i: (b, 0, 0)),
                   const2((1, GLA_DV)), const2((1, d)), const2((1, d)),
                   const2((d, d)), const2((d, 2 * FFN_HIDDEN)),
                   const2((SUBLANES, FFN_HIDDEN)), const2((1, FFN_HIDDEN)),
                   const2((FFN_HIDDEN, d))])
    return pl.pallas_call(
        _ffn_kernel,
        grid=(bsz, length // tm),
        in_specs=in_specs,
        out_specs=pl.BlockSpec((1, tm, d), lambda b, i: (b, i, 0)),
        out_shape=jax.ShapeDtypeStruct((bsz, length, d), F32),
        scratch_shapes=[pltpu.VMEM((tm + 2 * HALO, FFN_CHUNK), F32)],
        compiler_params=pltpu.CompilerParams(dimension_semantics=("arbitrary", "arbitrary"),
                                             vmem_limit_bytes=VMEM_LIMIT),
        name="out_proj_ffn",
    )(x, x, x, y_conv, y_conv, y_conv, o_f, o_f, o_f, o_b, o_b, o_b, og, og, og,
      mod, gn, n2, fg, w_out, w_up, dw, dwb, w_down)


def kernel(x, c, ctx, c_ctx, w_mod, b_mod, norm1_g, w_in, conv_dw, conv_b, conv_ln_g, conv_ln_b,
           w_gf, b_gf, w_gb, b_gb, gla_norm_g, w_out, norm2_g, w_up, ffn_dw, ffn_dw_b, w_down, final_g):
    bsz, seq, d = x.shape
    ctx_len = ctx.shape[1]
    layer = 0

    cvec = jnp.concatenate([c, c_ctx[None, :], jnp.zeros((SUBLANES - bsz - 1, d), F32)], axis=0)
    mods = _modulation(cvec, w_mod[layer], b_mod[layer][None, :])
    mod_lat = mods[:bsz][:, None, :]
    mod_ctx = jnp.broadcast_to(mods[bsz][None, None, :], (bsz, 1, N_MOD * d))

    w_in_p = jnp.pad(w_in[layer], ((0, 0), (0, D_IN_PAD - w_in.shape[2]))).astype(BF16)
    zeros_g = jnp.zeros((GATE_RANK, GLA_DK), F32)
    wgf = jnp.concatenate([w_gf[layer], zeros_g], axis=0).astype(BF16)
    wgb = jnp.concatenate([zeros_g, w_gb[layer]], axis=0).astype(BF16)
    bgf = b_gf[layer][None, :]
    bgb = b_gb[layer][None, :]
    n1 = norm1_g[layer][None, :]

    _, _, k_c, v_c, _, z_c = _input_proj(ctx, mod_ctx, n1, w_in_p, ctx_len)
    zero_state = jnp.zeros((bsz, GLA_HEADS, GLA_HEAD_DK, GLA_HEAD_DV), F32)
    _, _, s_f, s_b = _gla(k_c, k_c, v_c, z_c, wgf, bgf, wgb, bgb, zero_state, zero_state)

    glu, q, k, v, og, z = _input_proj(x, mod_lat, n1, w_in_p, FFN_TILE)
    o_f, o_b, _, _ = _gla(q, k, v, z, wgf, bgf, wgb, bgb, s_f, s_b)

    dw = jnp.pad(conv_dw[layer], ((0, 4 * SUBLANES - CONV_K), (0, 0)))
    y_conv = _conv_module(glu.reshape(bsz, seq // GRID_W, GRID_W, CONV_W),
                          dw[:, :CONV_HALF], dw[:, CONV_HALF:], conv_b[layer][None, :],
                          conv_ln_g[layer][None, :], conv_ln_b[layer][None, :])
    y_conv = y_conv.reshape(bsz, seq, CONV_W)

    ffn_w = jnp.pad(ffn_dw[layer], ((0, SUBLANES - ffn_dw.shape[1]), (0, 0)))
    return _ffn(x, y_conv, o_f, o_b, og, mod_lat, gla_norm_g[layer][None, :], norm2_g[layer][None, :],
                final_g[None, :], w_out[layer].astype(BF16), w_up[layer].astype(BF16),
                ffn_w, ffn_dw_b[layer][None, :], w_down[layer].astype(BF16))
```

```python
import functools

import jax
import jax.numpy as jnp
from jax import lax
from jax.experimental import pallas as pl
from jax.experimental.pallas import tpu as pltpu

F32 = jnp.float32
BF16 = jnp.bfloat16

D_MODEL = 1024
GRID_W = 64
CONV_W = 512
CONV_HALF = CONV_W // 2
CONV_K = 31
CONV_PAD = CONV_K // 2
GLA_HEADS = 4
GLA_DV = 512
GLA_HEAD_DV = 128
GLA_DK = 256
GLA_HEAD_DK = 64
GATE_RANK = 16
GATE_NORM = 16.0
CHUNK = 64
FFN_HIDDEN = 2816
N_MOD = 6
EPS = 1e-6

LANES = 128
SUBLANES = 8
V7X_VMEM_BYTES = 64 * 1024 * 1024
VMEM_LIMIT = V7X_VMEM_BYTES - 8 * 1024 * 1024

Z_PAD = LANES
D_IN_PAD = 2 * CONV_W + 2 * GLA_DK + 2 * GLA_DV + Z_PAD
OFF_CU, OFF_CG = 0, CONV_W
OFF_Q = 2 * CONV_W
OFF_K = OFF_Q + GLA_DK
OFF_V = OFF_K + GLA_DK
OFF_OG = OFF_V + GLA_DV
OFF_Z = OFF_OG + GLA_DV

GLA_BLOCK = 512
CONV_ROWS = 16
FFN_TILE = 512
FFN_CHUNK = 256
HALO = SUBLANES


def _sigmoid(x):
    return 1.0 / (1.0 + jnp.exp(-x))


def _split_bf16(x):
    hi = x.astype(BF16)
    lo = (x - hi.astype(F32)).astype(BF16)
    return hi, lo


def _dot(a, b):
    return jnp.dot(a, b, preferred_element_type=F32)


def _mod_kernel(c_ref, w_ref, b_ref, o_ref):
    c = c_ref[...]
    s = c * _sigmoid(c)
    s_hi, s_lo = _split_bf16(s)
    w_hi, w_lo = _split_bf16(w_ref[...])
    o_ref[...] = _dot(s_hi, w_hi) + _dot(s_lo, w_hi) + _dot(s_hi, w_lo) + b_ref[...]


def _modulation(cvec, w_mod, b_mod):
    rows, d = cvec.shape
    n = w_mod.shape[1]
    bn = D_MODEL
    return pl.pallas_call(
        _mod_kernel,
        grid=(n // bn,),
        in_specs=[pl.BlockSpec((rows, d), lambda j: (0, 0)),
                  pl.BlockSpec((d, bn), lambda j: (0, j)),
                  pl.BlockSpec((1, bn), lambda j: (0, j))],
        out_specs=pl.BlockSpec((rows, bn), lambda j: (0, j)),
        out_shape=jax.ShapeDtypeStruct((rows, n), F32),
        compiler_params=pltpu.CompilerParams(dimension_semantics=("arbitrary",),
                                             vmem_limit_bytes=VMEM_LIMIT),
        name="modulation",
    )(cvec, w_mod, b_mod)


def _in_kernel(x_ref, mod_ref, g_ref, w_ref, glu_ref, q_ref, k_ref, v_ref, og_ref, z_ref):
    x = x_ref[0]
    ms = jnp.mean(x * x, axis=-1, keepdims=True)
    y = x * lax.rsqrt(ms + EPS) * g_ref[...]
    m = mod_ref[0]
    shift = m[:, 0:D_MODEL]
    scale = m[:, D_MODEL:2 * D_MODEL]
    h = (y * (1.0 + scale) + shift).astype(BF16)

    def proj(lo, width):
        return _dot(h, w_ref[:, lo:lo + width])

    glu_ref[0] = proj(OFF_CU, CONV_W) * _sigmoid(proj(OFF_CG, CONV_W))
    q_ref[0] = proj(OFF_Q, GLA_DK)
    k_ref[0] = proj(OFF_K, GLA_DK)
    v_ref[0] = proj(OFF_V, GLA_DV)
    og_ref[0] = proj(OFF_OG, GLA_DV)
    z_ref[0] = proj(OFF_Z, Z_PAD)


def _input_proj(x, mod, norm_g, w_in_p, tm):
    bsz, length, d = x.shape
    widths = (CONV_W, GLA_DK, GLA_DK, GLA_DV, GLA_DV, Z_PAD)
    return pl.pallas_call(
        _in_kernel,
        grid=(bsz, length // tm),
        in_specs=[pl.BlockSpec((1, tm, d), lambda b, i: (b, i, 0)),
                  pl.BlockSpec((1, 1, N_MOD * d), lambda b, i: (b, 0, 0)),
                  pl.BlockSpec((1, d), lambda b, i: (0, 0)),
                  pl.BlockSpec((d, D_IN_PAD), lambda b, i: (0, 0))],
        out_specs=[pl.BlockSpec((1, tm, w), lambda b, i: (b, i, 0)) for w in widths],
        out_shape=[jax.ShapeDtypeStruct((bsz, length, w), F32) for w in widths],
        compiler_params=pltpu.CompilerParams(dimension_semantics=("arbitrary", "arbitrary"),
                                             vmem_limit_bytes=VMEM_LIMIT),
        name="input_proj",
    )(x, mod, norm_g, w_in_p)


def _gla_block(q_ref, k_ref, v_ref, z_ref, wg, bg, s_ref, o_ref, reverse):
    n_chunks = q_ref.shape[1] // CHUNK
    row = lax.broadcasted_iota(jnp.int32, (CHUNK, CHUNK), 0)
    col = lax.broadcasted_iota(jnp.int32, (CHUNK, CHUNK), 1)
    tri = ((col >= row) if reverse else (col <= row)).astype(BF16)
    row2 = lax.broadcasted_iota(jnp.int32, (CHUNK, LANES), 0)
    lane = lax.broadcasted_iota(jnp.int32, (CHUNK, LANES), 1)
    col2 = lane & (CHUNK - 1)
    keep_pair = (col2 >= row2) if reverse else (col2 <= row2)
    first = lane < GLA_HEAD_DK
    zero = jnp.zeros((CHUNK, LANES), F32)

    pre = _dot(z_ref[0, :, 0:2 * GATE_RANK].astype(BF16), wg) + bg
    g = (jnp.minimum(pre, 0.0) - jnp.log1p(jnp.exp(-jnp.abs(pre)))) * (1.0 / GATE_NORM)
    g_hi, g_lo = _split_bf16(g)
    b_parts, last_parts, decays = [], [], []
    for c in range(n_chunks):
        rows = slice(c * CHUNK, (c + 1) * CHUNK)
        b_c = _dot(tri, g_hi[rows]) + _dot(tri, g_lo[rows])
        b_last = b_c[0:1] if reverse else b_c[CHUNK - 1:CHUNK]
        b_parts.append(b_c)
        last_parts.append(jnp.broadcast_to(b_last, (CHUNK, GLA_DK)))
        decays.append(jnp.broadcast_to(jnp.exp(b_last), (LANES, GLA_DK)).T)
    b = jnp.concatenate(b_parts, axis=0)
    k = k_ref[0]
    q_e = (q_ref[0] * (GLA_HEAD_DK ** -0.5) * jnp.exp(b)).astype(BF16)
    k_e = k * jnp.exp(-b)
    k_t = (k * jnp.exp(jnp.concatenate(last_parts, axis=0) - b)).astype(BF16)
    v = v_ref[0]
    v_bf = v.astype(BF16)

    o_intra, kvs = [], []
    for c in range(n_chunks):
        rows = slice(c * CHUNK, (c + 1) * CHUNK)
        o_c, kv_c = [], []
        for p in range(GLA_HEADS // 2):
            lanes = slice(p * LANES, (p + 1) * LANES)
            wide = slice(2 * p * LANES, 2 * (p + 1) * LANES)
            kg = k_e[rows, lanes]
            k_blk = jnp.concatenate([jnp.where(first, kg, 0.0), jnp.where(first, 0.0, kg)],
                                    axis=0).astype(BF16)
            scores = lax.dot_general(q_e[rows, lanes], k_blk, (((1,), (1,)), ((), ())),
                                     preferred_element_type=F32)
            a = jnp.where(keep_pair, scores, 0.0).astype(BF16)
            vp = v[rows, wide]
            v_blk = jnp.concatenate(
                [jnp.concatenate([vp[:, :LANES], zero], axis=1),
                 jnp.concatenate([zero, vp[:, LANES:]], axis=1)], axis=0).astype(BF16)
            o_c.append(_dot(a, v_blk))
            kv_c.append(lax.dot_general(k_t[rows, lanes], v_bf[rows, wide],
                                        (((0,), (0,)), ((), ())), preferred_element_type=F32))
        o_intra.append(o_c)
        kvs.append(kv_c)

    state = [s_ref[h] for h in range(GLA_HEADS)]
    zero_s = jnp.zeros((GLA_HEAD_DK, GLA_HEAD_DV), F32)
    h0 = slice(0, GLA_HEAD_DK)
    h1 = slice(GLA_HEAD_DK, 2 * GLA_HEAD_DK)
    for c in (range(n_chunks - 1, -1, -1) if reverse else range(n_chunks)):
        rows = slice(c * CHUNK, (c + 1) * CHUNK)
        for p in range(GLA_HEADS // 2):
            lanes = slice(p * LANES, (p + 1) * LANES)
            wide = slice(2 * p * LANES, 2 * (p + 1) * LANES)
            s_blk = jnp.concatenate(
                [jnp.concatenate([state[2 * p], zero_s], axis=1),
                 jnp.concatenate([zero_s, state[2 * p + 1]], axis=1)], axis=0).astype(BF16)
            o_ref[0, rows, wide] = o_intra[c][p] + _dot(q_e[rows, lanes], s_blk)
            dp = decays[c][lanes]
            kv = kvs[c][p]
            state[2 * p] = dp[h0] * state[2 * p] + kv[h0, 0:LANES]
            state[2 * p + 1] = dp[h1] * state[2 * p + 1] + kv[h1, LANES:2 * LANES]
    for h in range(GLA_HEADS):
        s_ref[h] = state[h]


def _gla_kernel(qf_ref, kf_ref, vf_ref, zf_ref, qb_ref, kb_ref, vb_ref, zb_ref,
                wgf_ref, bgf_ref, wgb_ref, bgb_ref, sf0_ref, sb0_ref,
                of_ref, ob_ref, sf_out_ref, sb_out_ref, sf_ref, sb_ref):
    i = pl.program_id(1)

    @pl.when(i == 0)
    def _():
        sf_ref[...] = sf0_ref[0]
        sb_ref[...] = sb0_ref[0]

    _gla_block(qf_ref, kf_ref, vf_ref, zf_ref, wgf_ref[...], bgf_ref[...], sf_ref, of_ref, False)
    _gla_block(qb_ref, kb_ref, vb_ref, zb_ref, wgb_ref[...], bgb_ref[...], sb_ref, ob_ref, True)

    @pl.when(i == pl.num_programs(1) - 1)
    def _():
        sf_out_ref[0] = sf_ref[...]
        sb_out_ref[0] = sb_ref[...]


def _gla(q, k, v, z, wgf, bgf, wgb, bgb, sf0, sb0):
    bsz, length, _ = q.shape
    tb = min(GLA_BLOCK, length)
    assert length % tb == 0 and tb % CHUNK == 0
    nb = length // tb

    def fwd(w):
        return pl.BlockSpec((1, tb, w), lambda b, i: (b, i, 0))

    def bwd(w):
        return pl.BlockSpec((1, tb, w), lambda b, i: (b, nb - 1 - i, 0))

    def const2(shape):
        return pl.BlockSpec(shape, lambda b, i: (0, 0))

    state = pl.BlockSpec((1, GLA_HEADS, GLA_HEAD_DK, GLA_HEAD_DV), lambda b, i: (b, 0, 0, 0))
    state_shape = jax.ShapeDtypeStruct((bsz, GLA_HEADS, GLA_HEAD_DK, GLA_HEAD_DV), F32)
    o_shape = jax.ShapeDtypeStruct((bsz, length, GLA_DV), F32)
    return pl.pallas_call(
        _gla_kernel,
        grid=(bsz, nb),
        in_specs=[fwd(GLA_DK), fwd(GLA_DK), fwd(GLA_DV), fwd(Z_PAD),
                  bwd(GLA_DK), bwd(GLA_DK), bwd(GLA_DV), bwd(Z_PAD),
                  const2((2 * GATE_RANK, GLA_DK)), const2((1, GLA_DK)),
                  const2((2 * GATE_RANK, GLA_DK)), const2((1, GLA_DK)),
                  state, state],
        out_specs=[fwd(GLA_DV), bwd(GLA_DV), state, state],
        out_shape=[o_shape, o_shape, state_shape, state_shape],
        scratch_shapes=[pltpu.VMEM((GLA_HEADS, GLA_HEAD_DK, GLA_HEAD_DV), F32),
                        pltpu.VMEM((GLA_HEADS, GLA_HEAD_DK, GLA_HEAD_DV), F32)],
        compiler_params=pltpu.CompilerParams(dimension_semantics=("arbitrary", "arbitrary"),
                                             vmem_limit_bytes=VMEM_LIMIT),
        name="gla",
    )(q, k, v, z, q, k, v, z, wgf, bgf, wgb, bgb, sf0, sb0)


def _conv_kernel(grow_ref, gcol_ref, wrow_ref, wcol_ref, cb_ref, lg_ref, lb_ref, y_ref,
                 rowpad_ref, colpad_ref, yrow_ref):
    t = pl.program_id(1)
    n_rows = gcol_ref.shape[1]
    pad_rows = CONV_ROWS
    pad_w = 2 * SUBLANES

    @pl.when(t == 0)
    def _():
        rowpad_ref[...] = jnp.zeros(rowpad_ref.shape, F32)
        zeros = jnp.zeros((pad_rows, GRID_W, CONV_HALF), F32)
        colpad_ref[0:pad_rows] = zeros
        colpad_ref[pad_rows + n_rows:pad_rows + n_rows + pad_rows] = zeros

        def copy(j, carry):
            colpad_ref[pl.ds(pad_rows + j * CONV_ROWS, CONV_ROWS)] = gcol_ref[0, pl.ds(j * CONV_ROWS, CONV_ROWS)]
            return carry
        lax.fori_loop(0, n_rows // CONV_ROWS, copy, 0)

    for lg in range(CONV_HALF // LANES):
        rowpad_ref[lg, :, pad_w:pad_w + GRID_W, :] = grow_ref[0, :, :, lg * LANES:(lg + 1) * LANES]

    half_w = GRID_W // 2

    def row_body(r, carry):
        pieces = []
        for lg in range(CONV_HALF // LANES):
            lanes = slice(lg * LANES, (lg + 1) * LANES)
            for parity in range(2):
                acc = jnp.zeros((half_w, LANES), F32)
                for kk in range(CONV_K):
                    start = pad_w + kk - CONV_PAD + parity
                    acc = acc + (wrow_ref[kk:kk + 1, lanes]
                                 * rowpad_ref[lg, r, pl.ds(start, half_w, stride=2), :])
                yrow_ref[lg, pl.ds(parity, half_w, stride=2), :] = acc
            pieces.append(yrow_ref[lg])
        for lg in range(CONV_HALF // LANES):
            lanes = slice(lg * LANES, (lg + 1) * LANES)
            acc = jnp.zeros((GRID_W, LANES), F32)
            for kk in range(CONV_K):
                src = t * CONV_ROWS + r + (pad_rows + kk - CONV_PAD)
                acc = acc + wcol_ref[kk:kk + 1, lanes] * colpad_ref[src, :, lanes]
            pieces.append(acc)
        y = jnp.concatenate(pieces, axis=1) + cb_ref[...]
        mu = jnp.mean(y, axis=-1, keepdims=True)
        yc = y - mu
        var = jnp.mean(yc * yc, axis=-1, keepdims=True)
        yn = yc * lax.rsqrt(var + EPS) * lg_ref[...] + lb_ref[...]
        y_ref[0, r] = yn * _sigmoid(yn)
        return carry

    lax.fori_loop(0, CONV_ROWS, row_body, 0, unroll=2)


def _conv_module(glu4, w_row, w_col, conv_b, ln_g, ln_b):
    bsz, n_rows, gw, _ = glu4.shape
    return pl.pallas_call(
        _conv_kernel,
        grid=(bsz, n_rows // CONV_ROWS),
        in_specs=[pl.BlockSpec((1, CONV_ROWS, gw, CONV_HALF), lambda b, t: (b, t, 0, 0)),
                  pl.BlockSpec((1, n_rows, gw, CONV_HALF), lambda b, t: (b, 0, 0, 1)),
                  pl.BlockSpec((4 * SUBLANES, CONV_HALF), lambda b, t: (0, 0)),
                  pl.BlockSpec((4 * SUBLANES, CONV_HALF), lambda b, t: (0, 0)),
                  pl.BlockSpec((1, CONV_W), lambda b, t: (0, 0)),
                  pl.BlockSpec((1, CONV_W), lambda b, t: (0, 0)),
                  pl.BlockSpec((1, CONV_W), lambda b, t: (0, 0))],
        out_specs=pl.BlockSpec((1, CONV_ROWS, gw, CONV_W), lambda b, t: (b, t, 0, 0)),
        out_shape=jax.ShapeDtypeStruct((bsz, n_rows, gw, CONV_W), F32),
        scratch_shapes=[pltpu.VMEM((CONV_HALF // LANES, CONV_ROWS, gw + 4 * SUBLANES, LANES), F32),
                        pltpu.VMEM((n_rows + 2 * CONV_ROWS, gw, CONV_HALF), F32),
                        pltpu.VMEM((CONV_HALF // LANES, gw, LANES), F32)],
        compiler_params=pltpu.CompilerParams(dimension_semantics=("arbitrary", "arbitrary"),
                                             vmem_limit_bytes=VMEM_LIMIT),
        name="conv_module",
    )(glu4, glu4, w_row, w_col, conv_b, ln_g, ln_b)


def _ffn_kernel(x_p, x_m, x_n, yc_p, yc_m, yc_n, of_p, of_m, of_n, ob_p, ob_m, ob_n,
                og_p, og_m, og_n, mod_ref, gn_ref, n2_ref, fg_ref, wout_ref, wup_ref,
                dw_ref, dwb_ref, wdown_ref, out_ref, a_ref):
    i = pl.program_id(1)
    last = pl.num_programs(1) - 1
    tm = x_m.shape[1]
    main = slice(HALO, HALO + tm)

    def ext(p, m, n):
        return jnp.concatenate([p[0], m[0], n[0]], axis=0)

    mod = mod_ref[0]
    gate1 = mod[:, 2 * D_MODEL:3 * D_MODEL]
    shift2 = mod[:, 3 * D_MODEL:4 * D_MODEL]
    scale2 = mod[:, 4 * D_MODEL:5 * D_MODEL]
    gate2 = mod[:, 5 * D_MODEL:6 * D_MODEL]

    o = ext(of_p, of_m, of_n) + ext(ob_p, ob_m, ob_n)
    og = ext(og_p, og_m, og_n)
    parts = [ext(yc_p, yc_m, yc_n).astype(BF16)]
    for h in range(GLA_HEADS):
        lanes = slice(h * GLA_HEAD_DV, (h + 1) * GLA_HEAD_DV)
        oh = o[:, lanes]
        ms = jnp.mean(oh * oh, axis=-1, keepdims=True)
        ogh = og[:, lanes]
        parts.append((oh * lax.rsqrt(ms + EPS) * gn_ref[:, lanes] * (ogh * _sigmoid(ogh))).astype(BF16))
    mix = jnp.concatenate(parts, axis=1)
    x1 = ext(x_p, x_m, x_n) + gate1 * _dot(mix, wout_ref[...])

    ms = jnp.mean(x1 * x1, axis=-1, keepdims=True)
    h2 = ((x1 * lax.rsqrt(ms + EPS) * n2_ref[...]) * (1.0 + scale2) + shift2).astype(BF16)

    row = lax.broadcasted_iota(jnp.int32, (tm, 1), 0)
    kill_prev = jnp.logical_and(row == 0, i == 0)
    kill_next = jnp.logical_and(row == tm - 1, i == last)
    acc = jnp.zeros((tm, D_MODEL), F32)
    for j in range(FFN_HIDDEN // FFN_CHUNK):
        cols = slice(j * FFN_CHUNK, (j + 1) * FFN_CHUNK)
        vcols = slice(FFN_HIDDEN + j * FFN_CHUNK, FFN_HIDDEN + (j + 1) * FFN_CHUNK)
        a_ref[...] = _dot(h2, wup_ref[:, cols])
        val = _dot(h2, wup_ref[:, vcols])[main]
        a_prev = jnp.where(kill_prev, 0.0, a_ref[HALO - 1:HALO - 1 + tm, :])
        a_next = jnp.where(kill_next, 0.0, a_ref[HALO + 1:HALO + 1 + tm, :])
        ac = (dw_ref[0:1, cols] * a_prev + dw_ref[1:2, cols] * a_ref[main, :]
              + dw_ref[2:3, cols] * a_next + dwb_ref[:, cols])
        f = (ac * _sigmoid(ac) * val).astype(BF16)
        acc = acc + _dot(f, wdown_ref[cols, :])

    x2 = x1[main] + gate2 * acc
    ms = jnp.mean(x2 * x2, axis=-1, keepdims=True)
    out_ref[0] = x2 * lax.rsqrt(ms + EPS) * fg_ref[...]


def _ffn(x, y_conv, o_f, o_b, og, mod, gn, n2, fg, w_out, w_up, dw, dwb, w_down):
    bsz, length, d = x.shape
    tm = FFN_TILE
    per = tm // HALO
    n_halo = length // HALO

    def trio(w):
        return [pl.BlockSpec((1, HALO, w), lambda b, i: (b, jnp.maximum(i * per - 1, 0), 0)),
                pl.BlockSpec((1, tm, w), lambda b, i: (b, i, 0)),
                pl.BlockSpec((1, HALO, w), lambda b, i: (b, jnp.minimum((i + 1) * per, n_halo - 1), 0))]

    def const2(shape):
        return pl.BlockSpec(shape, lambda b, i: (0, 0))

    in_specs = (trio(d) + trio(CONV_W) + trio(GLA_DV) + trio(GLA_DV) + trio(GLA_DV)
                + [pl.BlockSpec((1, 1, N_MOD * d), lambda b, i: (b, 0, 0)),
                   const2((1, GLA_DV)), const2((1, d)), const2((1, d)),
                   const2((d, d)), const2((d, 2 * FFN_HIDDEN)),
                   const2((SUBLANES, FFN_HIDDEN)), const2((1, FFN_HIDDEN)),
                   const2((FFN_HIDDEN, d))])
    return pl.pallas_call(
        _ffn_kernel,
        grid=(bsz, length // tm),
        in_specs=in_specs,
        out_specs=pl.BlockSpec((1, tm, d), lambda b, i: (b, i, 0)),
        out_shape=jax.ShapeDtypeStruct((bsz, length, d), F32),
        scratch_shapes=[pltpu.VMEM((tm + 2 * HALO, FFN_CHUNK), F32)],
        compiler_params=pltpu.CompilerParams(dimension_semantics=("arbitrary", "arbitrary"),
                                             vmem_limit_bytes=VMEM_LIMIT),
        name="out_proj_ffn",
    )(x, x, x, y_conv, y_conv, y_conv, o_f, o_f, o_f, o_b, o_b, o_b, og, og, og,
      mod, gn, n2, fg, w_out, w_up, dw, dwb, w_down)


def kernel(x, c, ctx, c_ctx, w_mod, b_mod, norm1_g, w_in, conv_dw, conv_b, conv_ln_g, conv_ln_b,
           w_gf, b_gf, w_gb, b_gb, gla_norm_g, w_out, norm2_g, w_up, ffn_dw, ffn_dw_b, w_down, final_g):
    bsz, seq, d = x.shape
    ctx_len = ctx.shape[1]
    layer = 0

    cvec = jnp.concatenate([c, c_ctx[None, :], jnp.zeros((SUBLANES - bsz - 1, d), F32)], axis=0)
    mods = _modulation(cvec, w_mod[layer], b_mod[layer][None, :])
    mod_lat = mods[:bsz][:, None, :]
    mod_ctx = jnp.broadcast_to(mods[bsz][None, None, :], (bsz, 1, N_MOD * d))

    w_in_p = jnp.pad(w_in[layer], ((0, 0), (0, D_IN_PAD - w_in.shape[2]))).astype(BF16)
    zeros_g = jnp.zeros((GATE_RANK, GLA_DK), F32)
    wgf = jnp.concatenate([w_gf[layer], zeros_g], axis=0).astype(BF16)
    wgb = jnp.concatenate([zeros_g, w_gb[layer]], axis=0).astype(BF16)
    bgf = b_gf[layer][None, :]
    bgb = b_gb[layer][None, :]
    n1 = norm1_g[layer][None, :]

    _, _, k_c, v_c, _, z_c = _input_proj(ctx, mod_ctx, n1, w_in_p, ctx_len)
    zero_state = jnp.zeros((bsz, GLA_HEADS, GLA_HEAD_DK, GLA_HEAD_DV), F32)
    _, _, s_f, s_b = _gla(k_c, k_c, v_c, z_c, wgf, bgf, wgb, bgb, zero_state, zero_state)

    glu, q, k, v, og, z = _input_proj(x, mod_lat, n1, w_in_p, FFN_TILE)
    o_f, o_b, _, _ = _gla(q, k, v, z, wgf, bgf, wgb, bgb, s_f, s_b)

    dw = jnp.pad(conv_dw[layer], ((0, 4 * SUBLANES - CONV_K), (0, 0)))
    y_conv = _conv_module(glu.reshape(bsz, seq // GRID_W, GRID_W, CONV_W),
                          dw[:, :CONV_HALF], dw[:, CONV_HALF:], conv_b[layer][None, :],
                          conv_ln_g[layer][None, :], conv_ln_b[layer][None, :])
    y_conv = y_conv.reshape(bsz, seq, CONV_W)

    ffn_w = jnp.pad(ffn_dw[layer], ((0, SUBLANES - ffn_dw.shape[1]), (0, 0)))
    return _ffn(x, y_conv, o_f, o_b, og, mod_lat, gla_norm_g[layer][None, :], norm2_g[layer][None, :],
                final_g[None, :], w_out[layer].astype(BF16), w_up[layer].astype(BF16),
                ffn_w, ffn_dw_b[layer][None, :], w_down[layer].astype(BF16))
```

```python
import functools

import jax
import jax.numpy as jnp
from jax import lax
from jax.experimental import pallas as pl
from jax.experimental.pallas import tpu as pltpu

F32 = jnp.float32
BF16 = jnp.bfloat16

D_MODEL = 1024
GRID_W = 64
CONV_W = 512
CONV_HALF = CONV_W // 2
CONV_K = 31
CONV_PAD = CONV_K // 2
GLA_HEADS = 4
GLA_DV = 512
GLA_HEAD_DV = 128
GLA_DK = 256
GLA_HEAD_DK = 64
GATE_RANK = 16
GATE_NORM = 16.0
CHUNK = 64
FFN_HIDDEN = 2816
N_MOD = 6
EPS = 1e-6

LANES = 128
SUBLANES = 8
V7X_VMEM_BYTES = 64 * 1024 * 1024
VMEM_LIMIT = V7X_VMEM_BYTES - 8 * 1024 * 1024

Z_PAD = LANES
D_IN_PAD = 2 * CONV_W + 2 * GLA_DK + 2 * GLA_DV + Z_PAD
OFF_CU, OFF_CG = 0, CONV_W
OFF_Q = 2 * CONV_W
OFF_K = OFF_Q + GLA_DK
OFF_V = OFF_K + GLA_DK
OFF_OG = OFF_V + GLA_DV
OFF_Z = OFF_OG + GLA_DV

GLA_BLOCK = 512
CONV_ROWS = 16
FFN_TILE = 512
FFN_CHUNK = 256
HALO = SUBLANES


def _sigmoid(x):
    return 1.0 / (1.0 + jnp.exp(-x))


def _split_bf16(x):
    hi = x.astype(BF16)
    lo = (x - hi.astype(F32)).astype(BF16)
    return hi, lo


def _dot(a, b):
    return jnp.dot(a, b, preferred_element_type=F32)


def _mod_kernel(c_ref, w_ref, b_ref, o_ref):
    c = c_ref[...]
    s = c * _sigmoid(c)
    s_hi, s_lo = _split_bf16(s)
    w_hi, w_lo = _split_bf16(w_ref[...])
    o_ref[...] = _dot(s_hi, w_hi) + _dot(s_lo, w_hi) + _dot(s_hi, w_lo) + b_ref[...]


def _modulation(cvec, w_mod, b_mod):
    rows, d = cvec.shape
    n = w_mod.shape[1]
    bn = D_MODEL
    return pl.pallas_call(
        _mod_kernel,
        grid=(n // bn,),
        in_specs=[pl.BlockSpec((rows, d), lambda j: (0, 0)),
                  pl.BlockSpec((d, bn), lambda j: (0, j)),
                  pl.BlockSpec((1, bn), lambda j: (0, j))],
        out_specs=pl.BlockSpec((rows, bn), lambda j: (0, j)),
        out_shape=jax.ShapeDtypeStruct((rows, n), F32),
        compiler_params=pltpu.CompilerParams(dimension_semantics=("arbitrary",),
                                             vmem_limit_bytes=VMEM_LIMIT),
        name="modulation",
    )(cvec, w_mod, b_mod)


def _in_kernel(x_ref, mod_ref, g_ref, w_ref, glu_ref, q_ref, k_ref, v_ref, og_ref, z_ref):
    m = mod_ref[0]
    shift = m[:, 0:D_MODEL]
    scale = m[:, D_MODEL:2 * D_MODEL]
    tm = x_ref.shape[1]
    n_sub = 2 if tm % (4 * SUBLANES) == 0 else 1
    sub = tm // n_sub

    hs = []
    for s in range(n_sub):
        x = x_ref[0, s * sub:(s + 1) * sub, :]
        ms = jnp.mean(x * x, axis=-1, keepdims=True)
        y = x * lax.rsqrt(ms + EPS) * g_ref[...]
        hs.append((y * (1.0 + scale) + shift).astype(BF16))

    for s in range(n_sub):
        rows = slice(s * sub, (s + 1) * sub)

        def proj(lo, width):
            return _dot(hs[s], w_ref[:, lo:lo + width])

        glu_ref[0, rows, :] = (proj(OFF_CU, CONV_W) * _sigmoid(proj(OFF_CG, CONV_W))).astype(glu_ref.dtype)
        q_ref[0, rows, :] = proj(OFF_Q, GLA_DK).astype(q_ref.dtype)
        k_ref[0, rows, :] = proj(OFF_K, GLA_DK).astype(k_ref.dtype)
        v_ref[0, rows, :] = proj(OFF_V, GLA_DV).astype(v_ref.dtype)
        og_ref[0, rows, :] = proj(OFF_OG, GLA_DV).astype(og_ref.dtype)
        z_ref[0, rows, :] = proj(OFF_Z, Z_PAD).astype(z_ref.dtype)


def _input_proj(x, mod, norm_g, w_in_p, tm):
    bsz, length, d = x.shape
    widths = (CONV_W, GLA_DK, GLA_DK, GLA_DV, GLA_DV, Z_PAD)
    dtypes = (BF16, BF16, BF16, BF16, F32, BF16)
    return pl.pallas_call(
        _in_kernel,
        grid=(bsz, length // tm),
        in_specs=[pl.BlockSpec((1, tm, d), lambda b, i: (b, i, 0)),
                  pl.BlockSpec((1, 1, N_MOD * d), lambda b, i: (b, 0, 0)),
                  pl.BlockSpec((1, d), lambda b, i: (0, 0)),
                  pl.BlockSpec((d, D_IN_PAD), lambda b, i: (0, 0))],
        out_specs=[pl.BlockSpec((1, tm, w), lambda b, i: (b, i, 0)) for w in widths],
        out_shape=[jax.ShapeDtypeStruct((bsz, length, w), dt) for w, dt in zip(widths, dtypes)],
        compiler_params=pltpu.CompilerParams(dimension_semantics=("arbitrary", "arbitrary"),
                                             vmem_limit_bytes=VMEM_LIMIT),
        name="input_proj",
    )(x, mod, norm_g, w_in_p)


def _gla_block(q_ref, k_ref, v_ref, z_ref, wg, bg, s_ref, o_ref, reverse):
    n_chunks = q_ref.shape[1] // CHUNK
    row = lax.broadcasted_iota(jnp.int32, (CHUNK, CHUNK), 0)
    col = lax.broadcasted_iota(jnp.int32, (CHUNK, CHUNK), 1)
    tri = ((col >= row) if reverse else (col <= row)).astype(BF16)
    row2 = lax.broadcasted_iota(jnp.int32, (CHUNK, LANES), 0)
    lane = lax.broadcasted_iota(jnp.int32, (CHUNK, LANES), 1)
    col2 = lane & (CHUNK - 1)
    keep_pair = (col2 >= row2) if reverse else (col2 <= row2)
    first = lane < GLA_HEAD_DK
    zero = jnp.zeros((CHUNK, LANES), BF16)

    pre = _dot(z_ref[0, :, 0:2 * GATE_RANK].astype(BF16), wg) + bg
    g = (jnp.minimum(pre, 0.0) - jnp.log1p(jnp.exp(-jnp.abs(pre)))) * (1.0 / GATE_NORM)
    g_hi, g_lo = _split_bf16(g)
    b_parts, last_parts, decays = [], [], []
    for c in range(n_chunks):
        rows = slice(c * CHUNK, (c + 1) * CHUNK)
        b_c = _dot(tri, g_hi[rows]) + _dot(tri, g_lo[rows])
        b_last = b_c[0:1] if reverse else b_c[CHUNK - 1:CHUNK]
        b_parts.append(b_c)
        last_parts.append(jnp.broadcast_to(b_last, (CHUNK, GLA_DK)))
        decays.append(jnp.broadcast_to(jnp.exp(b_last), (LANES, GLA_DK)).T)
    b = jnp.concatenate(b_parts, axis=0)
    k = k_ref[0].astype(F32)
    q_e = (q_ref[0].astype(F32) * (GLA_HEAD_DK ** -0.5) * jnp.exp(b)).astype(BF16)
    k_e = k * jnp.exp(-b)
    k_t = (k * jnp.exp(jnp.concatenate(last_parts, axis=0) - b)).astype(BF16)
    v = v_ref[0].astype(BF16)

    o_intra, kvs = [], []
    for c in range(n_chunks):
        rows = slice(c * CHUNK, (c + 1) * CHUNK)
        o_c, kv_c = [], []
        for p in range(GLA_HEADS // 2):
            lanes = slice(p * LANES, (p + 1) * LANES)
            wide = slice(2 * p * LANES, 2 * (p + 1) * LANES)
            kg = k_e[rows, lanes]
            k_blk = jnp.concatenate([jnp.where(first, kg, 0.0), jnp.where(first, 0.0, kg)],
                                    axis=0).astype(BF16)
            scores = lax.dot_general(q_e[rows, lanes], k_blk, (((1,), (1,)), ((), ())),
                                     preferred_element_type=F32)
            a = jnp.where(keep_pair, scores, 0.0).astype(BF16)
            vp = v[rows, wide]
            v_blk = jnp.concatenate(
                [jnp.concatenate([vp[:, :LANES], zero], axis=1),
                 jnp.concatenate([zero, vp[:, LANES:]], axis=1)], axis=0)
            o_c.append(_dot(a, v_blk))
            kv_c.append(lax.dot_general(k_t[rows, lanes], vp,
                                        (((0,), (0,)), ((), ())), preferred_element_type=F32))
        o_intra.append(o_c)
        kvs.append(kv_c)

    state = [s_ref[h] for h in range(GLA_HEADS)]
    zero_s = jnp.zeros((GLA_HEAD_DK, GLA_HEAD_DV), F32)
    h0 = slice(0, GLA_HEAD_DK)
    h1 = slice(GLA_HEAD_DK, 2 * GLA_HEAD_DK)
    for c in (range(n_chunks - 1, -1, -1) if reverse else range(n_chunks)):
        rows = slice(c * CHUNK, (c + 1) * CHUNK)
        for p in range(GLA_HEADS // 2):
            lanes = slice(p * LANES, (p + 1) * LANES)
            wide = slice(2 * p * LANES, 2 * (p + 1) * LANES)
            s_blk = jnp.concatenate(
                [jnp.concatenate([state[2 * p], zero_s], axis=1),
                 jnp.concatenate([zero_s, state[2 * p + 1]], axis=1)], axis=0).astype(BF16)
            o_ref[0, rows, wide] = o_intra[c][p] + _dot(q_e[rows, lanes], s_blk)
            dp = decays[c][lanes]
            kv = kvs[c][p]
            state[2 * p] = dp[h0] * state[2 * p] + kv[h0, 0:LANES]
            state[2 * p + 1] = dp[h1] * state[2 * p + 1] + kv[h1, LANES:2 * LANES]
    for h in range(GLA_HEADS):
        s_ref[h] = state[h]


def _gla_kernel(qf_ref, kf_ref, vf_ref, zf_ref, qb_ref, kb_ref, vb_ref, zb_ref,
                wgf_ref, bgf_ref, wgb_ref, bgb_ref, sf0_ref, sb0_ref,
                of_ref, ob_ref, sf_out_ref, sb_out_ref, sf_ref, sb_ref):
    i = pl.program_id(1)

    @pl.when(i == 0)
    def _():
        sf_ref[...] = sf0_ref[0]
        sb_ref[...] = sb0_ref[0]

    _gla_block(qf_ref, kf_ref, vf_ref, zf_ref, wgf_ref[...], bgf_ref[...], sf_ref, of_ref, False)
    _gla_block(qb_ref, kb_ref, vb_ref, zb_ref, wgb_ref[...], bgb_ref[...], sb_ref, ob_ref, True)

    @pl.when(i == pl.num_programs(1) - 1)
    def _():
        sf_out_ref[0] = sf_ref[...]
        sb_out_ref[0] = sb_ref[...]


def _gla(q, k, v, z, wgf, bgf, wgb, bgb, sf0, sb0):
    bsz, length, _ = q.shape
    tb = min(GLA_BLOCK, length)
    assert length % tb == 0 and tb % CHUNK == 0
    nb = length // tb

    def fwd(w):
        return pl.BlockSpec((1, tb, w), lambda b, i: (b, i, 0))

    def bwd(w):
        return pl.BlockSpec((1, tb, w), lambda b, i: (b, nb - 1 - i, 0))

    def const2(shape):
        return pl.BlockSpec(shape, lambda b, i: (0, 0))

    state = pl.BlockSpec((1, GLA_HEADS, GLA_HEAD_DK, GLA_HEAD_DV), lambda b, i: (b, 0, 0, 0))
    state_shape = jax.ShapeDtypeStruct((bsz, GLA_HEADS, GLA_HEAD_DK, GLA_HEAD_DV), F32)
    o_shape = jax.ShapeDtypeStruct((bsz, length, GLA_DV), F32)
    return pl.pallas_call(
        _gla_kernel,
        grid=(bsz, nb),
        in_specs=[fwd(GLA_DK), fwd(GLA_DK), fwd(GLA_DV), fwd(Z_PAD),
                  bwd(GLA_DK), bwd(GLA_DK), bwd(GLA_DV), bwd(Z_PAD),
                  const2((2 * GATE_RANK, GLA_DK)), const2((1, GLA_DK)),
                  const2((2 * GATE_RANK, GLA_DK)), const2((1, GLA_DK)),
                  state, state],
        out_specs=[fwd(GLA_DV), bwd(GLA_DV), state, state],
        out_shape=[o_shape, o_shape, state_shape, state_shape],
        scratch_shapes=[pltpu.VMEM((GLA_HEADS, GLA_HEAD_DK, GLA_HEAD_DV), F32),
                        pltpu.VMEM((GLA_HEADS, GLA_HEAD_DK, GLA_HEAD_DV), F32)],
        compiler_params=pltpu.CompilerParams(dimension_semantics=("arbitrary", "arbitrary"),
                                             vmem_limit_bytes=VMEM_LIMIT),
        name="gla",
    )(q, k, v, z, q, k, v, z, wgf, bgf, wgb, bgb, sf0, sb0)


def _conv_kernel(grow_ref, gcol_ref, wrow_ref, wcol_ref, cb_ref, lg_ref, lb_ref, y_ref,
                 rowpad_ref, colpad_ref, yrow_ref):
    t = pl.program_id(1)
    n_rows = gcol_ref.shape[1]
    pad_rows = CONV_ROWS
    pad_w = 2 * SUBLANES

    @pl.when(t == 0)
    def _():
        rowpad_ref[...] = jnp.zeros(rowpad_ref.shape, F32)
        zeros = jnp.zeros((pad_rows, GRID_W, CONV_HALF), F32)
        colpad_ref[0:pad_rows] = zeros
        colpad_ref[pad_rows + n_rows:pad_rows + n_rows + pad_rows] = zeros

        def copy(j, carry):
            colpad_ref[pl.ds(pad_rows + j * CONV_ROWS, CONV_ROWS)] = (
                gcol_ref[0, pl.ds(j * CONV_ROWS, CONV_ROWS)].astype(F32))
            return carry
        lax.fori_loop(0, n_rows // CONV_ROWS, copy, 0)

    for lg in range(CONV_HALF // LANES):
        rowpad_ref[lg, :, pad_w:pad_w + GRID_W, :] = grow_ref[0, :, :, lg * LANES:(lg + 1) * LANES].astype(F32)

    half_w = GRID_W // 2

    def row_body(r, carry):
        pieces = []
        for lg in range(CONV_HALF // LANES):
            lanes = slice(lg * LANES, (lg + 1) * LANES)
            for parity in range(2):
                acc = jnp.zeros((half_w, LANES), F32)
                for kk in range(CONV_K):
                    start = pad_w + kk - CONV_PAD + parity
                    acc = acc + (wrow_ref[kk:kk + 1, lanes]
                                 * rowpad_ref[lg, r, pl.ds(start, half_w, stride=2), :])
                yrow_ref[lg, pl.ds(parity, half_w, stride=2), :] = acc
            pieces.append(yrow_ref[lg])
        for lg in range(CONV_HALF // LANES):
            lanes = slice(lg * LANES, (lg + 1) * LANES)
            acc = jnp.zeros((GRID_W, LANES), F32)
            for kk in range(CONV_K):
                src = t * CONV_ROWS + r + (pad_rows + kk - CONV_PAD)
                acc = acc + wcol_ref[kk:kk + 1, lanes] * colpad_ref[src, :, lanes]
            pieces.append(acc)
        y = jnp.concatenate(pieces, axis=1) + cb_ref[...]
        mu = jnp.mean(y, axis=-1, keepdims=True)
        yc = y - mu
        var = jnp.mean(yc * yc, axis=-1, keepdims=True)
        yn = yc * lax.rsqrt(var + EPS) * lg_ref[...] + lb_ref[...]
        y_ref[0, r] = yn * _sigmoid(yn)
        return carry

    lax.fori_loop(0, CONV_ROWS, row_body, 0, unroll=2)


def _conv_module(glu4, w_row, w_col, conv_b, ln_g, ln_b):
    bsz, n_rows, gw, _ = glu4.shape
    return pl.pallas_call(
        _conv_kernel,
        grid=(bsz, n_rows // CONV_ROWS),
        in_specs=[pl.BlockSpec((1, CONV_ROWS, gw, CONV_HALF), lambda b, t: (b, t, 0, 0)),
                  pl.BlockSpec((1, n_rows, gw, CONV_HALF), lambda b, t: (b, 0, 0, 1)),
                  pl.BlockSpec((4 * SUBLANES, CONV_HALF), lambda b, t: (0, 0)),
                  pl.BlockSpec((4 * SUBLANES, CONV_HALF), lambda b, t: (0, 0)),
                  pl.BlockSpec((1, CONV_W), lambda b, t: (0, 0)),
                  pl.BlockSpec((1, CONV_W), lambda b, t: (0, 0)),
                  pl.BlockSpec((1, CONV_W), lambda b, t: (0, 0))],
        out_specs=pl.BlockSpec((1, CONV_ROWS, gw, CONV_W), lambda b, t: (b, t, 0, 0)),
        out_shape=jax.ShapeDtypeStruct((bsz, n_rows, gw, CONV_W), F32),
        scratch_shapes=[pltpu.VMEM((CONV_HALF // LANES, CONV_ROWS, gw + 4 * SUBLANES, LANES), F32),
                        pltpu.VMEM((n_rows + 2 * CONV_ROWS, gw, CONV_HALF), F32),
                        pltpu.VMEM((CONV_HALF // LANES, gw, LANES), F32)],
        compiler_params=pltpu.CompilerParams(dimension_semantics=("arbitrary", "arbitrary"),
                                             vmem_limit_bytes=VMEM_LIMIT),
        name="conv_module",
    )(glu4, glu4, w_row, w_col, conv_b, ln_g, ln_b)


def _ffn_kernel(x_p, x_m, x_n, yc_p, yc_m, yc_n, of_p, of_m, of_n, ob_p, ob_m, ob_n,
                og_p, og_m, og_n, mod_ref, gn_ref, n2_ref, fg_ref, wout_ref, wup_ref,
                dw_ref, dwb_ref, wdown_ref, out_ref, a_ref):
    i = pl.program_id(1)
    last = pl.num_programs(1) - 1
    tm = x_m.shape[1]
    main = slice(HALO, HALO + tm)

    def ext(p, m, n):
        return jnp.concatenate([p[0], m[0], n[0]], axis=0)

    mod = mod_ref[0]
    gate1 = mod[:, 2 * D_MODEL:3 * D_MODEL]
    shift2 = mod[:, 3 * D_MODEL:4 * D_MODEL]
    scale2 = mod[:, 4 * D_MODEL:5 * D_MODEL]
    gate2 = mod[:, 5 * D_MODEL:6 * D_MODEL]

    o = ext(of_p, of_m, of_n) + ext(ob_p, ob_m, ob_n)
    og = ext(og_p, og_m, og_n)
    parts = [ext(yc_p, yc_m, yc_n).astype(BF16)]
    for h in range(GLA_HEADS):
        lanes = slice(h * GLA_HEAD_DV, (h + 1) * GLA_HEAD_DV)
        oh = o[:, lanes]
        ms = jnp.mean(oh * oh, axis=-1, keepdims=True)
        ogh = og[:, lanes]
        parts.append((oh * lax.rsqrt(ms + EPS) * gn_ref[:, lanes] * (ogh * _sigmoid(ogh))).astype(BF16))
    mix = jnp.concatenate(parts, axis=1)
    x1 = ext(x_p, x_m, x_n) + gate1 * _dot(mix, wout_ref[...])

    ms = jnp.mean(x1 * x1, axis=-1, keepdims=True)
    h2 = ((x1 * lax.rsqrt(ms + EPS) * n2_ref[...]) * (1.0 + scale2) + shift2).astype(BF16)

    row = lax.broadcasted_iota(jnp.int32, (tm, 1), 0)
    kill_prev = jnp.logical_and(row == 0, i == 0)
    kill_next = jnp.logical_and(row == tm - 1, i == last)
    n_chunks = FFN_HIDDEN // FFN_CHUNK

    def up(j):
        a_ref[j % 2] = _dot(h2, wup_ref[:, j * FFN_CHUNK:(j + 1) * FFN_CHUNK])
        return _dot(h2, wup_ref[:, FFN_HIDDEN + j * FFN_CHUNK:FFN_HIDDEN + (j + 1) * FFN_CHUNK])[main]

    acc = jnp.zeros((tm, D_MODEL), F32)
    val = up(0)
    for j in range(n_chunks):
        val_next = up(j + 1) if j + 1 < n_chunks else None
        cols = slice(j * FFN_CHUNK, (j + 1) * FFN_CHUNK)
        a_cur = a_ref.at[j % 2]
        a_prev = jnp.where(kill_prev, 0.0, a_cur[HALO - 1:HALO - 1 + tm, :])
        a_next = jnp.where(kill_next, 0.0, a_cur[HALO + 1:HALO + 1 + tm, :])
        ac = (dw_ref[0:1, cols] * a_prev + dw_ref[1:2, cols] * a_cur[main, :]
              + dw_ref[2:3, cols] * a_next + dwb_ref[:, cols])
        f = (ac * _sigmoid(ac) * val).astype(BF16)
        acc = acc + _dot(f, wdown_ref[cols, :])
        val = val_next

    x2 = x1[main] + gate2 * acc
    ms = jnp.mean(x2 * x2, axis=-1, keepdims=True)
    out_ref[0] = x2 * lax.rsqrt(ms + EPS) * fg_ref[...]


def _ffn(x, y_conv, o_f, o_b, og, mod, gn, n2, fg, w_out, w_up, dw, dwb, w_down):
    bsz, length, d = x.shape
    tm = FFN_TILE
    per = tm // HALO
    n_halo = length // HALO

    def trio(w):
        return [pl.BlockSpec((1, HALO, w), lambda b, i: (b, jnp.maximum(i * per - 1, 0), 0)),
                pl.BlockSpec((1, tm, w), lambda b, i: (b, i, 0)),
                pl.BlockSpec((1, HALO, w), lambda b, i: (b, jnp.minimum((i + 1) * per, n_halo - 1), 0))]

    def const2(shape):
        return pl.BlockSpec(shape, lambda b, i: (0, 0))

    in_specs = (trio(d) + trio(CONV_W) + trio(GLA_DV) + trio(GLA_DV) + trio(GLA_DV)
                + [pl.BlockSpec((1, 1, N_MOD * d), lambda b, i: (b, 0, 0)),
                   const2((1, GLA_DV)), const2((1, d)), const2((1, d)),
                   const2((d, d)), const2((d, 2 * FFN_HIDDEN)),
                   const2((SUBLANES, FFN_HIDDEN)), const2((1, FFN_HIDDEN)),
                   const2((FFN_HIDDEN, d))])
    return pl.pallas_call(
        _ffn_kernel,
        grid=(bsz, length // tm),
        in_specs=in_specs,
        out_specs=pl.BlockSpec((1, tm, d), lambda b, i: (b, i, 0)),
        out_shape=jax.ShapeDtypeStruct((bsz, length, d), F32),
        scratch_shapes=[pltpu.VMEM((2, tm + 2 * HALO, FFN_CHUNK), F32)],
        compiler_params=pltpu.CompilerParams(dimension_semantics=("arbitrary", "arbitrary"),
                                             vmem_limit_bytes=VMEM_LIMIT),
        name="out_proj_ffn",
    )(x, x, x, y_conv, y_conv, y_conv, o_f, o_f, o_f, o_b, o_b, o_b, og, og, og,
      mod, gn, n2, fg, w_out, w_up, dw, dwb, w_down)


def kernel(x, c, ctx, c_ctx, w_mod, b_mod, norm1_g, w_in, conv_dw, conv_b, conv_ln_g, conv_ln_b,
           w_gf, b_gf, w_gb, b_gb, gla_norm_g, w_out, norm2_g, w_up, ffn_dw, ffn_dw_b, w_down, final_g):
    bsz, seq, d = x.shape
    ctx_len = ctx.shape[1]
    layer = 0

    cvec = jnp.concatenate([c, c_ctx[None, :], jnp.zeros((SUBLANES - bsz - 1, d), F32)], axis=0)
    mods = _modulation(cvec, w_mod[layer], b_mod[layer][None, :])
    mod_lat = mods[:bsz][:, None, :]
    mod_ctx = jnp.broadcast_to(mods[bsz][None, None, :], (bsz, 1, N_MOD * d))

    w_in_p = jnp.pad(w_in[layer], ((0, 0), (0, D_IN_PAD - w_in.shape[2]))).astype(BF16)
    zeros_g = jnp.zeros((GATE_RANK, GLA_DK), F32)
    wgf = jnp.concatenate([w_gf[layer], zeros_g], axis=0).astype(BF16)
    wgb = jnp.concatenate([zeros_g, w_gb[layer]], axis=0).astype(BF16)
    bgf = b_gf[layer][None, :]
    bgb = b_gb[layer][None, :]
    n1 = norm1_g[layer][None, :]

    _, _, k_c, v_c, _, z_c = _input_proj(ctx, mod_ctx, n1, w_in_p, ctx_len)
    zero_state = jnp.zeros((bsz, GLA_HEADS, GLA_HEAD_DK, GLA_HEAD_DV), F32)
    _, _, s_f, s_b = _gla(k_c, k_c, v_c, z_c, wgf, bgf, wgb, bgb, zero_state, zero_state)

    glu, q, k, v, og, z = _input_proj(x, mod_lat, n1, w_in_p, FFN_TILE)
    o_f, o_b, _, _ = _gla(q, k, v, z, wgf, bgf, wgb, bgb, s_f, s_b)

    dw = jnp.pad(conv_dw[layer], ((0, 4 * SUBLANES - CONV_K), (0, 0)))
    y_conv = _conv_module(glu.reshape(bsz, seq // GRID_W, GRID_W, CONV_W),
                          dw[:, :CONV_HALF], dw[:, CONV_HALF:], conv_b[layer][None, :],
                          conv_ln_g[layer][None, :], conv_ln_b[layer][None, :])
    y_conv = y_conv.reshape(bsz, seq, CONV_W)

    ffn_w = jnp.pad(ffn_dw[layer], ((0, SUBLANES - ffn_dw.shape[1]), (0, 0)))
    return _ffn(x, y_conv, o_f, o_b, og, mod_lat, gla_norm_g[layer][None, :], norm2_g[layer][None, :],
                final_g[None, :], w_out[layer].astype(BF16), w_up[layer].astype(BF16),
                ffn_w, ffn_dw_b[layer][None, :], w_down[layer].astype(BF16))
```

```python
import jax
import jax.numpy as jnp
from jax import lax
from jax.experimental import pallas as pl
from jax.experimental.pallas import tpu as pltpu

F32 = jnp.float32
BF16 = jnp.bfloat16

D_MODEL = 1024
GRID_W = 64
CONV_W = 512
CONV_HALF = CONV_W // 2
CONV_K = 31
CONV_PAD = CONV_K // 2
GLA_HEADS = 4
GLA_DV = 512
GLA_HEAD_DV = 128
GLA_DK = 256
GLA_HEAD_DK = 64
GATE_RANK = 16
GATE_NORM = 16.0
CHUNK = 64
FFN_HIDDEN = 2816
N_MOD = 6
EPS = 1e-6

LANES = 128
SUBLANES = 8
BF16_ROWS = 2 * SUBLANES
V7X_VMEM_BYTES = 64 * 1024 * 1024
VMEM_LIMIT = V7X_VMEM_BYTES - 8 * 1024 * 1024

Z_PAD = LANES
D_IN_PAD = 2 * CONV_W + 2 * GLA_DK + 2 * GLA_DV + Z_PAD
OFF_CU, OFF_CG = 0, CONV_W
OFF_Q = 2 * CONV_W
OFF_K = OFF_Q + GLA_DK
OFF_V = OFF_K + GLA_DK
OFF_OG = OFF_V + GLA_DV
OFF_Z = OFF_OG + GLA_DV

IN_TILE = 1024
IN_SUB = 256
GLA_BLOCK = 1024
CONV_ROWS = 16
FFN_TILE = 512
FFN_CHUNK = 256
HALO = SUBLANES


def _sigmoid(x):
    return 1.0 / (1.0 + jnp.exp(-x))


def _split_bf16(x):
    hi = x.astype(BF16)
    lo = (x - hi.astype(F32)).astype(BF16)
    return hi, lo


def _dot(a, b):
    return jnp.dot(a, b, preferred_element_type=F32)


def _mod_kernel(c_ref, w_ref, b_ref, o_ref):
    c = c_ref[...]
    s = c * _sigmoid(c)
    s_hi, s_lo = _split_bf16(s)
    w_hi, w_lo = _split_bf16(w_ref[...])
    o_ref[...] = _dot(s_hi, w_hi) + _dot(s_lo, w_hi) + _dot(s_hi, w_lo) + b_ref[...]


def _modulation(cvec, w_mod, b_mod):
    rows, d = cvec.shape
    n = w_mod.shape[1]
    bn = D_MODEL
    return pl.pallas_call(
        _mod_kernel,
        grid=(n // bn,),
        in_specs=[pl.BlockSpec((rows, d), lambda j: (0, 0)),
                  pl.BlockSpec((d, bn), lambda j: (0, j)),
                  pl.BlockSpec((1, bn), lambda j: (0, j))],
        out_specs=pl.BlockSpec((rows, bn), lambda j: (0, j)),
        out_shape=jax.ShapeDtypeStruct((rows, n), F32),
        compiler_params=pltpu.CompilerParams(dimension_semantics=("arbitrary",),
                                             vmem_limit_bytes=VMEM_LIMIT),
        name="modulation",
    )(cvec, w_mod, b_mod)


def _in_kernel(x_ref, mod_ref, g_ref, w_ref, glu_ref, q_ref, k_ref, v_ref, og_ref, z_ref):
    m = mod_ref[0]
    shift = m[:, 0:D_MODEL]
    scale = m[:, D_MODEL:2 * D_MODEL]
    tm = x_ref.shape[1]
    sub = min(tm, IN_SUB)
    n_sub = tm // sub

    hs = []
    for s in range(n_sub):
        x = x_ref[0, s * sub:(s + 1) * sub, :]
        ms = jnp.mean(x * x, axis=-1, keepdims=True)
        y = x * lax.rsqrt(ms + EPS) * g_ref[...]
        hs.append((y * (1.0 + scale) + shift).astype(BF16))

    for s in range(n_sub):
        rows = slice(s * sub, (s + 1) * sub)

        def proj(lo, width):
            return _dot(hs[s], w_ref[:, lo:lo + width])

        glu_ref[0, rows, :] = (proj(OFF_CU, CONV_W) * _sigmoid(proj(OFF_CG, CONV_W))).astype(glu_ref.dtype)
        q_ref[0, rows, :] = proj(OFF_Q, GLA_DK).astype(q_ref.dtype)
        k_ref[0, rows, :] = proj(OFF_K, GLA_DK).astype(k_ref.dtype)
        v_ref[0, rows, :] = proj(OFF_V, GLA_DV).astype(v_ref.dtype)
        og_ref[0, rows, :] = proj(OFF_OG, GLA_DV).astype(og_ref.dtype)
        z_ref[0, rows, :] = proj(OFF_Z, Z_PAD).astype(z_ref.dtype)


def _input_proj(x, mod, norm_g, w_in_p, tm):
    bsz, length, d = x.shape
    widths = (CONV_W, GLA_DK, GLA_DK, GLA_DV, GLA_DV, Z_PAD)
    dtypes = (BF16, BF16, BF16, BF16, F32, BF16)
    return pl.pallas_call(
        _in_kernel,
        grid=(bsz, length // tm),
        in_specs=[pl.BlockSpec((1, tm, d), lambda b, i: (b, i, 0)),
                  pl.BlockSpec((1, 1, N_MOD * d), lambda b, i: (b, 0, 0)),
                  pl.BlockSpec((1, d), lambda b, i: (0, 0)),
                  pl.BlockSpec((d, D_IN_PAD), lambda b, i: (0, 0))],
        out_specs=[pl.BlockSpec((1, tm, w), lambda b, i: (b, i, 0)) for w in widths],
        out_shape=[jax.ShapeDtypeStruct((bsz, length, w), dt) for w, dt in zip(widths, dtypes)],
        compiler_params=pltpu.CompilerParams(dimension_semantics=("arbitrary", "arbitrary"),
                                             vmem_limit_bytes=VMEM_LIMIT),
        name="input_proj",
    )(x, mod, norm_g, w_in_p)


def _gla_block(q_ref, k_ref, v_ref, z_ref, wg, bg, s_ref, o_ref, reverse):
    n_chunks = q_ref.shape[1] // CHUNK
    row = lax.broadcasted_iota(jnp.int32, (CHUNK, CHUNK), 0)
    col = lax.broadcasted_iota(jnp.int32, (CHUNK, CHUNK), 1)
    tri = ((col >= row) if reverse else (col <= row)).astype(BF16)
    row2 = lax.broadcasted_iota(jnp.int32, (CHUNK, LANES), 0)
    lane = lax.broadcasted_iota(jnp.int32, (CHUNK, LANES), 1)
    col2 = lane & (CHUNK - 1)
    keep_pair = (col2 >= row2) if reverse else (col2 <= row2)
    first = lane < GLA_HEAD_DK
    zero = jnp.zeros((CHUNK, LANES), BF16)

    pre = _dot(z_ref[0, :, 0:2 * GATE_RANK].astype(BF16), wg) + bg
    g = (jnp.minimum(pre, 0.0) - jnp.log1p(jnp.exp(-jnp.abs(pre)))) * (1.0 / GATE_NORM)
    g_hi, g_lo = _split_bf16(g)
    b_parts, last_parts, decays = [], [], []
    for c in range(n_chunks):
        rows = slice(c * CHUNK, (c + 1) * CHUNK)
        b_c = _dot(tri, g_hi[rows]) + _dot(tri, g_lo[rows])
        b_last = b_c[0:1] if reverse else b_c[CHUNK - 1:CHUNK]
        b_parts.append(b_c)
        last_parts.append(jnp.broadcast_to(b_last, (CHUNK, GLA_DK)))
        decays.append(jnp.broadcast_to(jnp.exp(b_last), (LANES, GLA_DK)).T)
    b = jnp.concatenate(b_parts, axis=0)
    k = k_ref[0].astype(F32)
    q_e = (q_ref[0].astype(F32) * (GLA_HEAD_DK ** -0.5) * jnp.exp(b)).astype(BF16)
    k_e = k * jnp.exp(-b)
    k_t = (k * jnp.exp(jnp.concatenate(last_parts, axis=0) - b)).astype(BF16)
    v = v_ref[0].astype(BF16)

    o_intra, kvs = [], []
    for c in range(n_chunks):
        rows = slice(c * CHUNK, (c + 1) * CHUNK)
        o_c, kv_c = [], []
        for p in range(GLA_HEADS // 2):
            lanes = slice(p * LANES, (p + 1) * LANES)
            wide = slice(2 * p * LANES, 2 * (p + 1) * LANES)
            kg = k_e[rows, lanes]
            k_blk = jnp.concatenate([jnp.where(first, kg, 0.0), jnp.where(first, 0.0, kg)],
                                    axis=0).astype(BF16)
            scores = lax.dot_general(q_e[rows, lanes], k_blk, (((1,), (1,)), ((), ())),
                                     preferred_element_type=F32)
            a = jnp.where(keep_pair, scores, 0.0).astype(BF16)
            vp = v[rows, wide]
            v_blk = jnp.concatenate(
                [jnp.concatenate([vp[:, :LANES], zero], axis=1),
                 jnp.concatenate([zero, vp[:, LANES:]], axis=1)], axis=0)
            o_c.append(_dot(a, v_blk))
            kv_c.append(lax.dot_general(k_t[rows, lanes], vp,
                                        (((0,), (0,)), ((), ())), preferred_element_type=F32))
        o_intra.append(o_c)
        kvs.append(kv_c)

    state = [s_ref[h] for h in range(GLA_HEADS)]
    zero_s = jnp.zeros((GLA_HEAD_DK, GLA_HEAD_DV), F32)
    h0 = slice(0, GLA_HEAD_DK)
    h1 = slice(GLA_HEAD_DK, 2 * GLA_HEAD_DK)
    for c in (range(n_chunks - 1, -1, -1) if reverse else range(n_chunks)):
        rows = slice(c * CHUNK, (c + 1) * CHUNK)
        for p in range(GLA_HEADS // 2):
            lanes = slice(p * LANES, (p + 1) * LANES)
            wide = slice(2 * p * LANES, 2 * (p + 1) * LANES)
            s_blk = jnp.concatenate(
                [jnp.concatenate([state[2 * p], zero_s], axis=1),
                 jnp.concatenate([zero_s, state[2 * p + 1]], axis=1)], axis=0).astype(BF16)
            o_ref[0, rows, wide] = (o_intra[c][p] + _dot(q_e[rows, lanes], s_blk)).astype(o_ref.dtype)
            dp = decays[c][lanes]
            kv = kvs[c][p]
            state[2 * p] = dp[h0] * state[2 * p] + kv[h0, 0:LANES]
            state[2 * p + 1] = dp[h1] * state[2 * p + 1] + kv[h1, LANES:2 * LANES]
    for h in range(GLA_HEADS):
        s_ref[h] = state[h]


def _gla_kernel(qf_ref, kf_ref, vf_ref, zf_ref, qb_ref, kb_ref, vb_ref, zb_ref,
                wgf_ref, bgf_ref, wgb_ref, bgb_ref, sf0_ref, sb0_ref,
                of_ref, ob_ref, sf_out_ref, sb_out_ref, sf_ref, sb_ref):
    i = pl.program_id(1)

    @pl.when(i == 0)
    def _():
        sf_ref[...] = sf0_ref[0]
        sb_ref[...] = sb0_ref[0]

    _gla_block(qf_ref, kf_ref, vf_ref, zf_ref, wgf_ref[...], bgf_ref[...], sf_ref, of_ref, False)
    _gla_block(qb_ref, kb_ref, vb_ref, zb_ref, wgb_ref[...], bgb_ref[...], sb_ref, ob_ref, True)

    @pl.when(i == pl.num_programs(1) - 1)
    def _():
        sf_out_ref[0] = sf_ref[...]
        sb_out_ref[0] = sb_ref[...]


def _gla(q, k, v, z, wgf, bgf, wgb, bgb, sf0, sb0):
    bsz, length, _ = q.shape
    tb = min(GLA_BLOCK, length)
    assert length % tb == 0 and tb % CHUNK == 0
    nb = length // tb

    def fwd(w):
        return pl.BlockSpec((1, tb, w), lambda b, i: (b, i, 0))

    def bwd(w):
        return pl.BlockSpec((1, tb, w), lambda b, i: (b, nb - 1 - i, 0))

    def const2(shape):
        return pl.BlockSpec(shape, lambda b, i: (0, 0))

    state = pl.BlockSpec((1, GLA_HEADS, GLA_HEAD_DK, GLA_HEAD_DV), lambda b, i: (b, 0, 0, 0))
    state_shape = jax.ShapeDtypeStruct((bsz, GLA_HEADS, GLA_HEAD_DK, GLA_HEAD_DV), F32)
    o_shape = jax.ShapeDtypeStruct((bsz, length, GLA_DV), F32)
    return pl.pallas_call(
        _gla_kernel,
        grid=(bsz, nb),
        in_specs=[fwd(GLA_DK), fwd(GLA_DK), fwd(GLA_DV), fwd(Z_PAD),
                  bwd(GLA_DK), bwd(GLA_DK), bwd(GLA_DV), bwd(Z_PAD),
                  const2((2 * GATE_RANK, GLA_DK)), const2((1, GLA_DK)),
                  const2((2 * GATE_RANK, GLA_DK)), const2((1, GLA_DK)),
                  state, state],
        out_specs=[fwd(GLA_DV), bwd(GLA_DV), state, state],
        out_shape=[o_shape, o_shape, state_shape, state_shape],
        scratch_shapes=[pltpu.VMEM((GLA_HEADS, GLA_HEAD_DK, GLA_HEAD_DV), F32),
                        pltpu.VMEM((GLA_HEADS, GLA_HEAD_DK, GLA_HEAD_DV), F32)],
        compiler_params=pltpu.CompilerParams(dimension_semantics=("arbitrary", "arbitrary"),
                                             vmem_limit_bytes=VMEM_LIMIT),
        name="gla",
    )(q, k, v, z, q, k, v, z, wgf, bgf, wgb, bgb, sf0, sb0)


def _conv_kernel(grow_ref, gcol_ref, wrow_ref, wcol_ref, cb_ref, lg_ref, lb_ref, y_ref,
                 rowpad_ref, colpad_ref, yrow_ref):
    t = pl.program_id(1)
    n_rows = gcol_ref.shape[1]
    pad_rows = CONV_ROWS
    pad_w = 2 * SUBLANES

    @pl.when(t == 0)
    def _():
        rowpad_ref[...] = jnp.zeros(rowpad_ref.shape, F32)
        zeros = jnp.zeros((pad_rows, GRID_W, CONV_HALF), F32)
        colpad_ref[0:pad_rows] = zeros
        colpad_ref[pad_rows + n_rows:pad_rows + n_rows + pad_rows] = zeros

        def copy(j, carry):
            colpad_ref[pl.ds(pad_rows + j * CONV_ROWS, CONV_ROWS)] = (
                gcol_ref[0, pl.ds(j * CONV_ROWS, CONV_ROWS)].astype(F32))
            return carry
        lax.fori_loop(0, n_rows // CONV_ROWS, copy, 0)

    for lg in range(CONV_HALF // LANES):
        rowpad_ref[lg, :, pad_w:pad_w + GRID_W, :] = grow_ref[0, :, :, lg * LANES:(lg + 1) * LANES].astype(F32)

    half_w = GRID_W // 2

    def row_body(r, carry):
        pieces = []
        for lg in range(CONV_HALF // LANES):
            lanes = slice(lg * LANES, (lg + 1) * LANES)
            for parity in range(2):
                acc = jnp.zeros((half_w, LANES), F32)
                for kk in range(CONV_K):
                    start = pad_w + kk - CONV_PAD + parity
                    acc = acc + (wrow_ref[kk:kk + 1, lanes]
                                 * rowpad_ref[lg, r, pl.ds(start, half_w, stride=2), :])
                yrow_ref[lg, pl.ds(parity, half_w, stride=2), :] = acc
            pieces.append(yrow_ref[lg])
        for lg in range(CONV_HALF // LANES):
            lanes = slice(lg * LANES, (lg + 1) * LANES)
            acc = jnp.zeros((GRID_W, LANES), F32)
            for kk in range(CONV_K):
                src = t * CONV_ROWS + r + (pad_rows + kk - CONV_PAD)
                acc = acc + wcol_ref[kk:kk + 1, lanes] * colpad_ref[src, :, lanes]
            pieces.append(acc)
        y = jnp.concatenate(pieces, axis=1) + cb_ref[...]
        mu = jnp.mean(y, axis=-1, keepdims=True)
        yc = y - mu
        var = jnp.mean(yc * yc, axis=-1, keepdims=True)
        yn = yc * lax.rsqrt(var + EPS) * lg_ref[...] + lb_ref[...]
        y_ref[0, r] = (yn * _sigmoid(yn)).astype(y_ref.dtype)
        return carry

    lax.fori_loop(0, CONV_ROWS, row_body, 0, unroll=2)


def _conv_module(glu4, w_row, w_col, conv_b, ln_g, ln_b):
    bsz, n_rows, gw, _ = glu4.shape
    return pl.pallas_call(
        _conv_kernel,
        grid=(bsz, n_rows // CONV_ROWS),
        in_specs=[pl.BlockSpec((1, CONV_ROWS, gw, CONV_HALF), lambda b, t: (b, t, 0, 0)),
                  pl.BlockSpec((1, n_rows, gw, CONV_HALF), lambda b, t: (b, 0, 0, 1)),
                  pl.BlockSpec((4 * SUBLANES, CONV_HALF), lambda b, t: (0, 0)),
                  pl.BlockSpec((4 * SUBLANES, CONV_HALF), lambda b, t: (0, 0)),
                  pl.BlockSpec((1, CONV_W), lambda b, t: (0, 0)),
                  pl.BlockSpec((1, CONV_W), lambda b, t: (0, 0)),
                  pl.BlockSpec((1, CONV_W), lambda b, t: (0, 0))],
        out_specs=pl.BlockSpec((1, CONV_ROWS, gw, CONV_W), lambda b, t: (b, t, 0, 0)),
        out_shape=jax.ShapeDtypeStruct((bsz, n_rows, gw, CONV_W), F32),
        scratch_shapes=[pltpu.VMEM((CONV_HALF // LANES, CONV_ROWS, gw + 4 * SUBLANES, LANES), F32),
                        pltpu.VMEM((n_rows + 2 * CONV_ROWS, gw, CONV_HALF), F32),
                        pltpu.VMEM((CONV_HALF // LANES, gw, LANES), F32)],
        compiler_params=pltpu.CompilerParams(dimension_semantics=("arbitrary", "arbitrary"),
                                             vmem_limit_bytes=VMEM_LIMIT),
        name="conv_module",
    )(glu4, glu4, w_row, w_col, conv_b, ln_g, ln_b)


def _ffn_kernel(x_p, x_m, x_n, yc_p, yc_m, yc_n, of_p, of_m, of_n, ob_p, ob_m, ob_n,
                og_p, og_m, og_n, mod_ref, gn_ref, n2_ref, fg_ref, wout_ref, wup_ref,
                dw_ref, dwb_ref, wdown_ref, out_ref,
                a_ref, v_ref, h2_ref, x1_ref, f_ref, dep_ref):
    i = pl.program_id(1)
    last = pl.num_programs(1) - 1
    tm = x_m.shape[1]
    half = tm // 2
    n_groups = D_MODEL // LANES
    n_chunks = FFN_HIDDEN // FFN_CHUNK
    chunk_groups = FFN_CHUNK // LANES

    def ext16(p, m, n):
        return jnp.concatenate([p[0].astype(F32)[BF16_ROWS - HALO:], m[0].astype(F32),
                                n[0].astype(F32)[:HALO]], axis=0)

    mod = mod_ref[0]
    gate1 = mod[:, 2 * D_MODEL:3 * D_MODEL]
    shift2 = mod[:, 3 * D_MODEL:4 * D_MODEL]
    scale2 = mod[:, 4 * D_MODEL:5 * D_MODEL]
    gate2 = mod[:, 5 * D_MODEL:6 * D_MODEL]

    o = ext16(of_p, of_m, of_n) + ext16(ob_p, ob_m, ob_n)
    og = ext16(og_p, og_m, og_n)
    parts = [ext16(yc_p, yc_m, yc_n).astype(BF16)]
    for h in range(GLA_HEADS):
        lanes = slice(h * GLA_HEAD_DV, (h + 1) * GLA_HEAD_DV)
        oh = o[:, lanes]
        ms = jnp.mean(oh * oh, axis=-1, keepdims=True)
        ogh = og[:, lanes]
        parts.append((oh * lax.rsqrt(ms + EPS) * gn_ref[:, lanes] * (ogh * _sigmoid(ogh))).astype(BF16))
    mix = jnp.concatenate(parts, axis=1)
    x1 = jnp.concatenate([x_p[0], x_m[0], x_n[0]], axis=0) + gate1 * _dot(mix, wout_ref[...])
    for g in range(n_groups):
        x1_ref[g] = x1[:, g * LANES:(g + 1) * LANES]
    ms = jnp.mean(x1 * x1, axis=-1, keepdims=True)
    h2_ref[...] = ((x1 * lax.rsqrt(ms + EPS) * n2_ref[...]) * (1.0 + scale2) + shift2).astype(BF16)

    lane_ok = lax.broadcasted_iota(jnp.int32, (1, LANES), 1) >= 0
    first_in_batch = jnp.logical_and(lane_ok, i == 0)
    final_in_batch = jnp.logical_and(lane_ok, i == last)

    def up(j):
        slot = j % 2
        a = _dot(h2_ref[...], wup_ref[:, j * FFN_CHUNK:(j + 1) * FFN_CHUNK])
        v = _dot(h2_ref[...], wup_ref[:, FFN_HIDDEN + j * FFN_CHUNK:FFN_HIDDEN + (j + 1) * FFN_CHUNK])
        for lg in range(chunk_groups):
            lanes = slice(lg * LANES, (lg + 1) * LANES)
            a_ref[slot, lg] = a[:, lanes]
            v_ref[slot, lg] = v[:, lanes]
            a_ref[slot, lg, HALO - 1:HALO, :] = jnp.where(first_in_batch, 0.0, a_ref[slot, lg, HALO - 1:HALO, :])
            a_ref[slot, lg, HALO + tm:HALO + tm + 1, :] = jnp.where(
                final_in_batch, 0.0, a_ref[slot, lg, HALO + tm:HALO + tm + 1, :])

    acc = jnp.zeros((tm, D_MODEL), F32)
    up(0)
    for j in range(n_chunks):
        if j + 1 < n_chunks:
            up(j + 1)
        slot = j % 2
        for lg in range(chunk_groups):
            cs = slice(j * FFN_CHUNK + lg * LANES, j * FFN_CHUNK + (lg + 1) * LANES)
            for parity in range(2):
                base = HALO + parity

                def rows(ref, off):
                    return ref[slot, lg, pl.ds(base + off, half, stride=2), :]

                ac = (dw_ref[0:1, cs] * rows(a_ref, -1) + dw_ref[1:2, cs] * rows(a_ref, 0)
                      + dw_ref[2:3, cs] * rows(a_ref, 1) + dwb_ref[:, cs])
                f_ref[slot, parity * half:(parity + 1) * half, lg * LANES:(lg + 1) * LANES] = (
                    ac * _sigmoid(ac) * rows(v_ref, 0)).astype(BF16)
        acc = acc + _dot(f_ref[slot], wdown_ref[j * FFN_CHUNK:(j + 1) * FFN_CHUNK, :])

    sq = jnp.zeros((tm, LANES), F32)
    for g in range(n_groups):
        cs = slice(g * LANES, (g + 1) * LANES)
        x1p = jnp.concatenate([x1_ref[g, pl.ds(HALO, half, stride=2), :],
                               x1_ref[g, pl.ds(HALO + 1, half, stride=2), :]], axis=0)
        x2 = x1p + gate2[:, cs] * acc[:, cs]
        sq = sq + x2 * x2
        x1_ref[g, 0:tm, :] = x2

    rinv = lax.rsqrt(jnp.sum(sq, axis=-1, keepdims=True) * (1.0 / D_MODEL) + EPS)
    for g in range(n_groups):
        cs = slice(g * LANES, (g + 1) * LANES)
        y = x1_ref[g, 0:tm, :] * rinv * fg_ref[:, cs]
        dep_ref[g % 2, pl.ds(0, half, stride=2), :] = y[0:half]
        dep_ref[g % 2, pl.ds(1, half, stride=2), :] = y[half:tm]
        out_ref[0, :, cs] = dep_ref[g % 2]


def _ffn(x, y_conv, o_f, o_b, og, mod, gn, n2, fg, w_out, w_up, dw, dwb, w_down):
    bsz, length, d = x.shape
    tm = FFN_TILE

    def trio(w, halo):
        per = tm // halo
        n_halo = length // halo
        return [pl.BlockSpec((1, halo, w), lambda b, i: (b, jnp.maximum(i * per - 1, 0), 0)),
                pl.BlockSpec((1, tm, w), lambda b, i: (b, i, 0)),
                pl.BlockSpec((1, halo, w), lambda b, i: (b, jnp.minimum((i + 1) * per, n_halo - 1), 0))]

    def const2(shape):
        return pl.BlockSpec(shape, lambda b, i: (0, 0))

    in_specs = (trio(d, HALO) + trio(CONV_W, BF16_ROWS) + trio(GLA_DV, BF16_ROWS)
                + trio(GLA_DV, BF16_ROWS) + trio(GLA_DV, BF16_ROWS)
                + [pl.BlockSpec((1, 1, N_MOD * d), lambda b, i: (b, 0, 0)),
                   const2((1, GLA_DV)), const2((1, d)), const2((1, d)),
                   const2((d, d)), const2((d, 2 * FFN_HIDDEN)),
                   const2((SUBLANES, FFN_HIDDEN)), const2((1, FFN_HIDDEN)),
                   const2((FFN_HIDDEN, d))])
    ext_rows = tm + 2 * HALO
    return pl.pallas_call(
        _ffn_kernel,
        grid=(bsz, length // tm),
        in_specs=in_specs,
        out_specs=pl.BlockSpec((1, tm, d), lambda b, i: (b, i, 0)),
        out_shape=jax.ShapeDtypeStruct((bsz, length, d), F32),
        scratch_shapes=[pltpu.VMEM((2, FFN_CHUNK // LANES, ext_rows, LANES), F32),
                        pltpu.VMEM((2, FFN_CHUNK // LANES, ext_rows, LANES), F32),
                        pltpu.VMEM((ext_rows, d), BF16),
                        pltpu.VMEM((d // LANES, ext_rows, LANES), F32),
                        pltpu.VMEM((2, tm, FFN_CHUNK), BF16),
                        pltpu.VMEM((2, tm, LANES), F32)],
        compiler_params=pltpu.CompilerParams(dimension_semantics=("arbitrary", "arbitrary"),
                                             vmem_limit_bytes=VMEM_LIMIT),
        name="out_proj_ffn",
    )(x, x, x, y_conv, y_conv, y_conv, o_f, o_f, o_f, o_b, o_b, o_b, og, og, og,
      mod, gn, n2, fg, w_out, w_up, dw, dwb, w_down)


def _ffn_plain_kernel(x_p, x_m, x_n, yc_p, yc_m, yc_n, of_p, of_m, of_n, ob_p, ob_m, ob_n,
                      og_p, og_m, og_n, mod_ref, gn_ref, n2_ref, fg_ref, wout_ref, wup_ref,
                      dw_ref, dwb_ref, wdown_ref, out_ref, a_ref):
    i = pl.program_id(1)
    last = pl.num_programs(1) - 1
    tm = x_m.shape[1]
    main = slice(HALO, HALO + tm)

    def ext(p, m, n):
        return jnp.concatenate([p[0], m[0], n[0]], axis=0)

    mod = mod_ref[0]
    gate1 = mod[:, 2 * D_MODEL:3 * D_MODEL]
    shift2 = mod[:, 3 * D_MODEL:4 * D_MODEL]
    scale2 = mod[:, 4 * D_MODEL:5 * D_MODEL]
    gate2 = mod[:, 5 * D_MODEL:6 * D_MODEL]

    o = ext(of_p, of_m, of_n) + ext(ob_p, ob_m, ob_n)
    og = ext(og_p, og_m, og_n)
    parts = [ext(yc_p, yc_m, yc_n).astype(BF16)]
    for h in range(GLA_HEADS):
        lanes = slice(h * GLA_HEAD_DV, (h + 1) * GLA_HEAD_DV)
        oh = o[:, lanes]
        ms = jnp.mean(oh * oh, axis=-1, keepdims=True)
        ogh = og[:, lanes]
        parts.append((oh * lax.rsqrt(ms + EPS) * gn_ref[:, lanes] * (ogh * _sigmoid(ogh))).astype(BF16))
    mix = jnp.concatenate(parts, axis=1)
    x1 = ext(x_p, x_m, x_n) + gate1 * _dot(mix, wout_ref[...])

    ms = jnp.mean(x1 * x1, axis=-1, keepdims=True)
    h2 = ((x1 * lax.rsqrt(ms + EPS) * n2_ref[...]) * (1.0 + scale2) + shift2).astype(BF16)

    row = lax.broadcasted_iota(jnp.int32, (tm, 1), 0)
    kill_prev = jnp.logical_and(row == 0, i == 0)
    kill_next = jnp.logical_and(row == tm - 1, i == last)
    n_chunks = FFN_HIDDEN // FFN_CHUNK

    def up(j):
        a_ref[j % 2] = _dot(h2, wup_ref[:, j * FFN_CHUNK:(j + 1) * FFN_CHUNK])
        return _dot(h2, wup_ref[:, FFN_HIDDEN + j * FFN_CHUNK:FFN_HIDDEN + (j + 1) * FFN_CHUNK])[main]

    acc = jnp.zeros((tm, D_MODEL), F32)
    val = up(0)
    for j in range(n_chunks):
        val_next = up(j + 1) if j + 1 < n_chunks else None
        cols = slice(j * FFN_CHUNK, (j + 1) * FFN_CHUNK)
        a_cur = a_ref.at[j % 2]
        a_prev = jnp.where(kill_prev, 0.0, a_cur[HALO - 1:HALO - 1 + tm, :])
        a_next = jnp.where(kill_next, 0.0, a_cur[HALO + 1:HALO + 1 + tm, :])
        ac = (dw_ref[0:1, cols] * a_prev + dw_ref[1:2, cols] * a_cur[main, :]
              + dw_ref[2:3, cols] * a_next + dwb_ref[:, cols])
        f = (ac * _sigmoid(ac) * val).astype(BF16)
        acc = acc + _dot(f, wdown_ref[cols, :])
        val = val_next

    x2 = x1[main] + gate2 * acc
    ms = jnp.mean(x2 * x2, axis=-1, keepdims=True)
    out_ref[0] = x2 * lax.rsqrt(ms + EPS) * fg_ref[...]


def _ffn_plain(x, y_conv, o_f, o_b, og, mod, gn, n2, fg, w_out, w_up, dw, dwb, w_down):
    bsz, length, d = x.shape
    tm = FFN_TILE
    per = tm // HALO
    n_halo = length // HALO

    def trio(w):
        return [pl.BlockSpec((1, HALO, w), lambda b, i: (b, jnp.maximum(i * per - 1, 0), 0)),
                pl.BlockSpec((1, tm, w), lambda b, i: (b, i, 0)),
                pl.BlockSpec((1, HALO, w), lambda b, i: (b, jnp.minimum((i + 1) * per, n_halo - 1), 0))]

    def const2(shape):
        return pl.BlockSpec(shape, lambda b, i: (0, 0))

    in_specs = (trio(d) + trio(CONV_W) + trio(GLA_DV) + trio(GLA_DV) + trio(GLA_DV)
                + [pl.BlockSpec((1, 1, N_MOD * d), lambda b, i: (b, 0, 0)),
                   const2((1, GLA_DV)), const2((1, d)), const2((1, d)),
                   const2((d, d)), const2((d, 2 * FFN_HIDDEN)),
                   const2((SUBLANES, FFN_HIDDEN)), const2((1, FFN_HIDDEN)),
                   const2((FFN_HIDDEN, d))])
    return pl.pallas_call(
        _ffn_plain_kernel,
        grid=(bsz, length // tm),
        in_specs=in_specs,
        out_specs=pl.BlockSpec((1, tm, d), lambda b, i: (b, i, 0)),
        out_shape=jax.ShapeDtypeStruct((bsz, length, d), F32),
        scratch_shapes=[pltpu.VMEM((2, tm + 2 * HALO, FFN_CHUNK), F32)],
        compiler_params=pltpu.CompilerParams(dimension_semantics=("arbitrary", "arbitrary"),
                                             vmem_limit_bytes=VMEM_LIMIT),
        name="out_proj_ffn",
    )(x, x, x, y_conv, y_conv, y_conv, o_f, o_f, o_f, o_b, o_b, o_b, og, og, og,
      mod, gn, n2, fg, w_out, w_up, dw, dwb, w_down)


def kernel(x, c, ctx, c_ctx, w_mod, b_mod, norm1_g, w_in, conv_dw, conv_b, conv_ln_g, conv_ln_b,
           w_gf, b_gf, w_gb, b_gb, gla_norm_g, w_out, norm2_g, w_up, ffn_dw, ffn_dw_b, w_down, final_g):
    bsz, seq, d = x.shape
    ctx_len = ctx.shape[1]
    layer = 0

    cvec = jnp.concatenate([c, c_ctx[None, :], jnp.zeros((SUBLANES - bsz - 1, d), F32)], axis=0)
    mods = _modulation(cvec, w_mod[layer], b_mod[layer][None, :])
    mod_lat = mods[:bsz][:, None, :]
    mod_ctx = jnp.broadcast_to(mods[bsz][None, None, :], (bsz, 1, N_MOD * d))

    w_in_p = jnp.pad(w_in[layer], ((0, 0), (0, D_IN_PAD - w_in.shape[2]))).astype(BF16)
    zeros_g = jnp.zeros((GATE_RANK, GLA_DK), F32)
    wgf = jnp.concatenate([w_gf[layer], zeros_g], axis=0).astype(BF16)
    wgb = jnp.concatenate([zeros_g, w_gb[layer]], axis=0).astype(BF16)
    bgf = b_gf[layer][None, :]
    bgb = b_gb[layer][None, :]
    n1 = norm1_g[layer][None, :]

    _, _, k_c, v_c, _, z_c = _input_proj(ctx, mod_ctx, n1, w_in_p, ctx_len)
    zero_state = jnp.zeros((bsz, GLA_HEADS, GLA_HEAD_DK, GLA_HEAD_DV), F32)
    _, _, s_f, s_b = _gla(k_c, k_c, v_c, z_c, wgf, bgf, wgb, bgb, zero_state, zero_state)

    glu, q, k, v, og, z = _input_proj(x, mod_lat, n1, w_in_p, IN_TILE)
    o_f, o_b, _, _ = _gla(q, k, v, z, wgf, bgf, wgb, bgb, s_f, s_b)

    dw = jnp.pad(conv_dw[layer], ((0, 4 * SUBLANES - CONV_K), (0, 0)))
    y_conv = _conv_module(glu.reshape(bsz, seq // GRID_W, GRID_W, CONV_W),
                          dw[:, :CONV_HALF], dw[:, CONV_HALF:], conv_b[layer][None, :],
                          conv_ln_g[layer][None, :], conv_ln_b[layer][None, :])
    y_conv = y_conv.reshape(bsz, seq, CONV_W)

    ffn_w = jnp.pad(ffn_dw[layer], ((0, SUBLANES - ffn_dw.shape[1]), (0, 0)))
    return _ffn_plain(x, y_conv, o_f, o_b, og, mod_lat, gla_norm_g[layer][None, :], norm2_g[layer][None, :],
                      final_g[None, :], w_out[layer].astype(BF16), w_up[layer].astype(BF16),
                      ffn_w, ffn_dw_b[layer][None, :], w_down[layer].astype(BF16))
```

```python
import jax
import jax.numpy as jnp
from jax import lax
from jax.experimental import pallas as pl
from jax.experimental.pallas import tpu as pltpu

F32 = jnp.float32
BF16 = jnp.bfloat16

D_MODEL = 1024
GRID_W = 64
CONV_W = 512
CONV_HALF = CONV_W // 2
CONV_K = 31
CONV_PAD = CONV_K // 2
GLA_HEADS = 4
GLA_DV = 512
GLA_HEAD_DV = 128
GLA_DK = 256
GLA_HEAD_DK = 64
GATE_RANK = 16
GATE_NORM = 16.0
CHUNK = 64
FFN_HIDDEN = 2816
N_MOD = 6
EPS = 1e-6

LANES = 128
SUBLANES = 8
V7X_VMEM_BYTES = 64 * 1024 * 1024
VMEM_LIMIT = V7X_VMEM_BYTES - 8 * 1024 * 1024

Z_PAD = LANES
D_IN_PAD = 2 * CONV_W + 2 * GLA_DK + 2 * GLA_DV + Z_PAD
OFF_CU, OFF_CG = 0, CONV_W
OFF_Q = 2 * CONV_W
OFF_K = OFF_Q + GLA_DK
OFF_V = OFF_K + GLA_DK
OFF_OG = OFF_V + GLA_DV
OFF_Z = OFF_OG + GLA_DV

IN_TILE = 1024
IN_SUB = 256
GLA_BLOCK = 1024
CONV_ROWS = 16
FFN_TILE = 512
FFN_CHUNK = 256
HALO = SUBLANES


def _sigmoid(x):
    return 1.0 / (1.0 + jnp.exp(-x))


def _split_bf16(x):
    hi = x.astype(BF16)
    lo = (x - hi.astype(F32)).astype(BF16)
    return hi, lo


def _dot(a, b):
    return jnp.dot(a, b, preferred_element_type=F32)


def _mod_kernel(c_ref, w_ref, b_ref, o_ref):
    c = c_ref[...]
    s = c * _sigmoid(c)
    s_hi, s_lo = _split_bf16(s)
    w_hi, w_lo = _split_bf16(w_ref[...])
    o_ref[...] = _dot(s_hi, w_hi) + _dot(s_lo, w_hi) + _dot(s_hi, w_lo) + b_ref[...]


def _modulation(cvec, w_mod, b_mod):
    rows, d = cvec.shape
    n = w_mod.shape[1]
    bn = D_MODEL
    return pl.pallas_call(
        _mod_kernel,
        grid=(n // bn,),
        in_specs=[pl.BlockSpec((rows, d), lambda j: (0, 0)),
                  pl.BlockSpec((d, bn), lambda j: (0, j)),
                  pl.BlockSpec((1, bn), lambda j: (0, j))],
        out_specs=pl.BlockSpec((rows, bn), lambda j: (0, j)),
        out_shape=jax.ShapeDtypeStruct((rows, n), F32),
        compiler_params=pltpu.CompilerParams(dimension_semantics=("arbitrary",),
                                             vmem_limit_bytes=VMEM_LIMIT),
        name="modulation",
    )(cvec, w_mod, b_mod)


def _in_kernel(x_ref, mod_ref, g_ref, w_ref, glu_ref, q_ref, k_ref, v_ref, og_ref, z_ref):
    m = mod_ref[0]
    shift = m[:, 0:D_MODEL]
    scale = m[:, D_MODEL:2 * D_MODEL]
    tm = x_ref.shape[1]
    sub = min(tm, IN_SUB)
    n_sub = tm // sub

    hs = []
    for s in range(n_sub):
        x = x_ref[0, s * sub:(s + 1) * sub, :]
        ms = jnp.mean(x * x, axis=-1, keepdims=True)
        y = x * lax.rsqrt(ms + EPS) * g_ref[...]
        hs.append((y * (1.0 + scale) + shift).astype(BF16))

    for s in range(n_sub):
        rows = slice(s * sub, (s + 1) * sub)

        def proj(lo, width):
            return _dot(hs[s], w_ref[:, lo:lo + width])

        glu_ref[0, rows, :] = (proj(OFF_CU, CONV_W) * _sigmoid(proj(OFF_CG, CONV_W))).astype(glu_ref.dtype)
        q_ref[0, rows, :] = proj(OFF_Q, GLA_DK).astype(q_ref.dtype)
        k_ref[0, rows, :] = proj(OFF_K, GLA_DK).astype(k_ref.dtype)
        v_ref[0, rows, :] = proj(OFF_V, GLA_DV).astype(v_ref.dtype)
        og_ref[0, rows, :] = proj(OFF_OG, GLA_DV).astype(og_ref.dtype)
        z_ref[0, rows, :] = proj(OFF_Z, Z_PAD).astype(z_ref.dtype)


def _input_proj(x, mod, norm_g, w_in_p, tm):
    bsz, length, d = x.shape
    widths = (CONV_W, GLA_DK, GLA_DK, GLA_DV, GLA_DV, Z_PAD)
    dtypes = (BF16, BF16, BF16, BF16, F32, BF16)
    return pl.pallas_call(
        _in_kernel,
        grid=(bsz, length // tm),
        in_specs=[pl.BlockSpec((1, tm, d), lambda b, i: (b, i, 0)),
                  pl.BlockSpec((1, 1, N_MOD * d), lambda b, i: (b, 0, 0)),
                  pl.BlockSpec((1, d), lambda b, i: (0, 0)),
                  pl.BlockSpec((d, D_IN_PAD), lambda b, i: (0, 0))],
        out_specs=[pl.BlockSpec((1, tm, w), lambda b, i: (b, i, 0)) for w in widths],
        out_shape=[jax.ShapeDtypeStruct((bsz, length, w), dt) for w, dt in zip(widths, dtypes)],
        compiler_params=pltpu.CompilerParams(dimension_semantics=("arbitrary", "arbitrary"),
                                             vmem_limit_bytes=VMEM_LIMIT),
        name="input_proj",
    )(x, mod, norm_g, w_in_p)


def _gla_block(q_ref, k_ref, v_ref, z_ref, wg, bg, s_ref, o_ref, reverse):
    n_chunks = q_ref.shape[1] // CHUNK
    row = lax.broadcasted_iota(jnp.int32, (CHUNK, CHUNK), 0)
    col = lax.broadcasted_iota(jnp.int32, (CHUNK, CHUNK), 1)
    tri = ((col >= row) if reverse else (col <= row)).astype(BF16)
    row2 = lax.broadcasted_iota(jnp.int32, (CHUNK, LANES), 0)
    lane = lax.broadcasted_iota(jnp.int32, (CHUNK, LANES), 1)
    col2 = lane & (CHUNK - 1)
    keep_pair = (col2 >= row2) if reverse else (col2 <= row2)
    first = lane < GLA_HEAD_DK
    zero = jnp.zeros((CHUNK, LANES), BF16)

    pre = _dot(z_ref[0, :, 0:2 * GATE_RANK].astype(BF16), wg) + bg
    g = (jnp.minimum(pre, 0.0) - jnp.log1p(jnp.exp(-jnp.abs(pre)))) * (1.0 / GATE_NORM)
    g_hi, g_lo = _split_bf16(g)
    b_parts, last_parts, decays = [], [], []
    for c in range(n_chunks):
        rows = slice(c * CHUNK, (c + 1) * CHUNK)
        b_c = _dot(tri, g_hi[rows]) + _dot(tri, g_lo[rows])
        b_last = b_c[0:1] if reverse else b_c[CHUNK - 1:CHUNK]
        b_parts.append(b_c)
        last_parts.append(jnp.broadcast_to(b_last, (CHUNK, GLA_DK)))
        decays.append(jnp.broadcast_to(jnp.exp(b_last), (LANES, GLA_DK)).T)
    b = jnp.concatenate(b_parts, axis=0)
    k = k_ref[0].astype(F32)
    q_e = (q_ref[0].astype(F32) * (GLA_HEAD_DK ** -0.5) * jnp.exp(b)).astype(BF16)
    k_e = k * jnp.exp(-b)
    k_t = (k * jnp.exp(jnp.concatenate(last_parts, axis=0) - b)).astype(BF16)
    v = v_ref[0].astype(BF16)

    o_intra, kvs = [], []
    for c in range(n_chunks):
        rows = slice(c * CHUNK, (c + 1) * CHUNK)
        o_c, kv_c = [], []
        for p in range(GLA_HEADS // 2):
            lanes = slice(p * LANES, (p + 1) * LANES)
            wide = slice(2 * p * LANES, 2 * (p + 1) * LANES)
            kg = k_e[rows, lanes]
            k_blk = jnp.concatenate([jnp.where(first, kg, 0.0), jnp.where(first, 0.0, kg)],
                                    axis=0).astype(BF16)
            scores = lax.dot_general(q_e[rows, lanes], k_blk, (((1,), (1,)), ((), ())),
                                     preferred_element_type=F32)
            a = jnp.where(keep_pair, scores, 0.0).astype(BF16)
            vp = v[rows, wide]
            v_blk = jnp.concatenate(
                [jnp.concatenate([vp[:, :LANES], zero], axis=1),
                 jnp.concatenate([zero, vp[:, LANES:]], axis=1)], axis=0)
            o_c.append((a, v_blk))
            kv_c.append(lax.dot_general(k_t[rows, lanes], vp,
                                        (((0,), (0,)), ((), ())), preferred_element_type=F32))
        o_intra.append(o_c)
        kvs.append(kv_c)

    state = [s_ref[h] for h in range(GLA_HEADS)]
    zero_s = jnp.zeros((GLA_HEAD_DK, GLA_HEAD_DV), F32)
    h0 = slice(0, GLA_HEAD_DK)
    h1 = slice(GLA_HEAD_DK, 2 * GLA_HEAD_DK)
    for c in (range(n_chunks - 1, -1, -1) if reverse else range(n_chunks)):
        rows = slice(c * CHUNK, (c + 1) * CHUNK)
        for p in range(GLA_HEADS // 2):
            lanes = slice(p * LANES, (p + 1) * LANES)
            wide = slice(2 * p * LANES, 2 * (p + 1) * LANES)
            s_blk = jnp.concatenate(
                [jnp.concatenate([state[2 * p], zero_s], axis=1),
                 jnp.concatenate([zero_s, state[2 * p + 1]], axis=1)], axis=0).astype(BF16)
            a, v_blk = o_intra[c][p]
            o_ref[0, rows, wide] = _dot(jnp.concatenate([q_e[rows, lanes], a], axis=1),
                                        jnp.concatenate([s_blk, v_blk], axis=0)).astype(o_ref.dtype)
            dp = decays[c][lanes]
            kv = kvs[c][p]
            state[2 * p] = dp[h0] * state[2 * p] + kv[h0, 0:LANES]
            state[2 * p + 1] = dp[h1] * state[2 * p + 1] + kv[h1, LANES:2 * LANES]
    for h in range(GLA_HEADS):
        s_ref[h] = state[h]


def _gla_kernel(qf_ref, kf_ref, vf_ref, zf_ref, qb_ref, kb_ref, vb_ref, zb_ref,
                wgf_ref, bgf_ref, wgb_ref, bgb_ref, sf0_ref, sb0_ref,
                of_ref, ob_ref, sf_out_ref, sb_out_ref, sf_ref, sb_ref):
    i = pl.program_id(1)

    @pl.when(i == 0)
    def _():
        sf_ref[...] = sf0_ref[0]
        sb_ref[...] = sb0_ref[0]

    _gla_block(qf_ref, kf_ref, vf_ref, zf_ref, wgf_ref[...], bgf_ref[...], sf_ref, of_ref, False)
    _gla_block(qb_ref, kb_ref, vb_ref, zb_ref, wgb_ref[...], bgb_ref[...], sb_ref, ob_ref, True)

    @pl.when(i == pl.num_programs(1) - 1)
    def _():
        sf_out_ref[0] = sf_ref[...]
        sb_out_ref[0] = sb_ref[...]


def _gla(q, k, v, z, wgf, bgf, wgb, bgb, sf0, sb0):
    bsz, length, _ = q.shape
    tb = min(GLA_BLOCK, length)
    assert length % tb == 0 and tb % CHUNK == 0
    nb = length // tb

    def fwd(w):
        return pl.BlockSpec((1, tb, w), lambda b, i: (b, i, 0))

    def bwd(w):
        return pl.BlockSpec((1, tb, w), lambda b, i: (b, nb - 1 - i, 0))

    def const2(shape):
        return pl.BlockSpec(shape, lambda b, i: (0, 0))

    state = pl.BlockSpec((1, GLA_HEADS, GLA_HEAD_DK, GLA_HEAD_DV), lambda b, i: (b, 0, 0, 0))
    state_shape = jax.ShapeDtypeStruct((bsz, GLA_HEADS, GLA_HEAD_DK, GLA_HEAD_DV), F32)
    o_shape = jax.ShapeDtypeStruct((bsz, length, GLA_DV), F32)
    return pl.pallas_call(
        _gla_kernel,
        grid=(bsz, nb),
        in_specs=[fwd(GLA_DK), fwd(GLA_DK), fwd(GLA_DV), fwd(Z_PAD),
                  bwd(GLA_DK), bwd(GLA_DK), bwd(GLA_DV), bwd(Z_PAD),
                  const2((2 * GATE_RANK, GLA_DK)), const2((1, GLA_DK)),
                  const2((2 * GATE_RANK, GLA_DK)), const2((1, GLA_DK)),
                  state, state],
        out_specs=[fwd(GLA_DV), bwd(GLA_DV), state, state],
        out_shape=[o_shape, o_shape, state_shape, state_shape],
        scratch_shapes=[pltpu.VMEM((GLA_HEADS, GLA_HEAD_DK, GLA_HEAD_DV), F32),
                        pltpu.VMEM((GLA_HEADS, GLA_HEAD_DK, GLA_HEAD_DV), F32)],
        compiler_params=pltpu.CompilerParams(dimension_semantics=("arbitrary", "arbitrary"),
                                             vmem_limit_bytes=VMEM_LIMIT),
        name="gla",
    )(q, k, v, z, q, k, v, z, wgf, bgf, wgb, bgb, sf0, sb0)


def _conv_kernel(grow_ref, gcol_ref, wrow_ref, wcol_ref, cb_ref, lg_ref, lb_ref, y_ref,
                 rowpad_ref, colpad_ref, yrow_ref):
    t = pl.program_id(1)
    n_rows = gcol_ref.shape[1]
    pad_rows = CONV_ROWS
    pad_w = 2 * SUBLANES

    @pl.when(t == 0)
    def _():
        rowpad_ref[...] = jnp.zeros(rowpad_ref.shape, F32)
        zeros = jnp.zeros((pad_rows, GRID_W, CONV_HALF), F32)
        colpad_ref[0:pad_rows] = zeros
        colpad_ref[pad_rows + n_rows:pad_rows + n_rows + pad_rows] = zeros

        def copy(j, carry):
            colpad_ref[pl.ds(pad_rows + j * CONV_ROWS, CONV_ROWS)] = (
                gcol_ref[0, pl.ds(j * CONV_ROWS, CONV_ROWS)].astype(F32))
            return carry
        lax.fori_loop(0, n_rows // CONV_ROWS, copy, 0)

    for lg in range(CONV_HALF // LANES):
        rowpad_ref[lg, :, pad_w:pad_w + GRID_W, :] = grow_ref[0, :, :, lg * LANES:(lg + 1) * LANES].astype(F32)

    half_w = GRID_W // 2

    def row_body(r, carry):
        pieces = []
        for lg in range(CONV_HALF // LANES):
            lanes = slice(lg * LANES, (lg + 1) * LANES)
            for parity in range(2):
                acc = jnp.broadcast_to(cb_ref[:, lanes], (half_w, LANES))
                for kk in range(CONV_K):
                    start = pad_w + kk - CONV_PAD + parity
                    acc = acc + (wrow_ref[kk:kk + 1, lanes]
                                 * rowpad_ref[lg, r, pl.ds(start, half_w, stride=2), :])
                yrow_ref[lg, pl.ds(parity, half_w, stride=2), :] = acc
            pieces.append(yrow_ref[lg])
        for lg in range(CONV_HALF // LANES):
            lanes = slice(lg * LANES, (lg + 1) * LANES)
            acc = jnp.broadcast_to(cb_ref[:, CONV_HALF + lg * LANES:CONV_HALF + (lg + 1) * LANES],
                                   (GRID_W, LANES))
            for kk in range(CONV_K):
                src = t * CONV_ROWS + r + (pad_rows + kk - CONV_PAD)
                acc = acc + wcol_ref[kk:kk + 1, lanes] * colpad_ref[src, :, lanes]
            pieces.append(acc)
        y = jnp.concatenate(pieces, axis=1)
        mu = jnp.mean(y, axis=-1, keepdims=True)
        var = jnp.mean(y * y, axis=-1, keepdims=True) - mu * mu
        yn = (y - mu) * lax.rsqrt(var + EPS) * lg_ref[...] + lb_ref[...]
        y_ref[0, r] = (yn * _sigmoid(yn)).astype(y_ref.dtype)
        return carry

    lax.fori_loop(0, CONV_ROWS, row_body, 0, unroll=2)


def _conv_module(glu4, w_row, w_col, conv_b, ln_g, ln_b):
    bsz, n_rows, gw, _ = glu4.shape
    return pl.pallas_call(
        _conv_kernel,
        grid=(bsz, n_rows // CONV_ROWS),
        in_specs=[pl.BlockSpec((1, CONV_ROWS, gw, CONV_HALF), lambda b, t: (b, t, 0, 0)),
                  pl.BlockSpec((1, n_rows, gw, CONV_HALF), lambda b, t: (b, 0, 0, 1)),
                  pl.BlockSpec((4 * SUBLANES, CONV_HALF), lambda b, t: (0, 0)),
                  pl.BlockSpec((4 * SUBLANES, CONV_HALF), lambda b, t: (0, 0)),
                  pl.BlockSpec((1, CONV_W), lambda b, t: (0, 0)),
                  pl.BlockSpec((1, CONV_W), lambda b, t: (0, 0)),
                  pl.BlockSpec((1, CONV_W), lambda b, t: (0, 0))],
        out_specs=pl.BlockSpec((1, CONV_ROWS, gw, CONV_W), lambda b, t: (b, t, 0, 0)),
        out_shape=jax.ShapeDtypeStruct((bsz, n_rows, gw, CONV_W), F32),
        scratch_shapes=[pltpu.VMEM((CONV_HALF // LANES, CONV_ROWS, gw + 4 * SUBLANES, LANES), F32),
                        pltpu.VMEM((n_rows + 2 * CONV_ROWS, gw, CONV_HALF), F32),
                        pltpu.VMEM((CONV_HALF // LANES, gw, LANES), F32)],
        compiler_params=pltpu.CompilerParams(dimension_semantics=("arbitrary", "arbitrary"),
                                             vmem_limit_bytes=VMEM_LIMIT),
        name="conv_module",
    )(glu4, glu4, w_row, w_col, conv_b, ln_g, ln_b)


def _ffn_kernel(x_p, x_m, x_n, yc_p, yc_m, yc_n, of_p, of_m, of_n, ob_p, ob_m, ob_n,
                      og_p, og_m, og_n, mod_ref, gn_ref, n2_ref, fg_ref, wout_ref, wup_ref,
                      dw_ref, dwb_ref, wdown_ref, out_ref, a_ref):
    i = pl.program_id(1)
    last = pl.num_programs(1) - 1
    tm = x_m.shape[1]
    main = slice(HALO, HALO + tm)

    def ext(p, m, n):
        return jnp.concatenate([p[0], m[0], n[0]], axis=0)

    mod = mod_ref[0]
    gate1 = mod[:, 2 * D_MODEL:3 * D_MODEL]
    shift2 = mod[:, 3 * D_MODEL:4 * D_MODEL]
    scale2 = mod[:, 4 * D_MODEL:5 * D_MODEL]
    gate2 = mod[:, 5 * D_MODEL:6 * D_MODEL]

    o = ext(of_p, of_m, of_n) + ext(ob_p, ob_m, ob_n)
    og = ext(og_p, og_m, og_n)
    parts = [ext(yc_p, yc_m, yc_n).astype(BF16)]
    for h in range(GLA_HEADS):
        lanes = slice(h * GLA_HEAD_DV, (h + 1) * GLA_HEAD_DV)
        oh = o[:, lanes]
        ms = jnp.mean(oh * oh, axis=-1, keepdims=True)
        ogh = og[:, lanes]
        parts.append((oh * lax.rsqrt(ms + EPS) * gn_ref[:, lanes] * (ogh * _sigmoid(ogh))).astype(BF16))
    mix = jnp.concatenate(parts, axis=1)
    x1 = ext(x_p, x_m, x_n) + gate1 * _dot(mix, wout_ref[...])

    ms = jnp.mean(x1 * x1, axis=-1, keepdims=True)
    h2 = ((x1 * lax.rsqrt(ms + EPS) * n2_ref[...]) * (1.0 + scale2) + shift2).astype(BF16)

    row = lax.broadcasted_iota(jnp.int32, (tm, 1), 0)
    kill_prev = jnp.logical_and(row == 0, i == 0)
    kill_next = jnp.logical_and(row == tm - 1, i == last)
    n_chunks = FFN_HIDDEN // FFN_CHUNK

    def up(j):
        a_ref[j % 2] = _dot(h2, wup_ref[:, j * FFN_CHUNK:(j + 1) * FFN_CHUNK])
        return _dot(h2, wup_ref[:, FFN_HIDDEN + j * FFN_CHUNK:FFN_HIDDEN + (j + 1) * FFN_CHUNK])[main]

    acc = jnp.zeros((tm, D_MODEL), F32)
    val = up(0)
    for j in range(n_chunks):
        val_next = up(j + 1) if j + 1 < n_chunks else None
        cols = slice(j * FFN_CHUNK, (j + 1) * FFN_CHUNK)
        a_cur = a_ref.at[j % 2]
        a_prev = jnp.where(kill_prev, 0.0, a_cur[HALO - 1:HALO - 1 + tm, :])
        a_next = jnp.where(kill_next, 0.0, a_cur[HALO + 1:HALO + 1 + tm, :])
        ac = (dw_ref[0:1, cols] * a_prev + dw_ref[1:2, cols] * a_cur[main, :]
              + dw_ref[2:3, cols] * a_next + dwb_ref[:, cols])
        f = (ac * _sigmoid(ac) * val).astype(BF16)
        acc = acc + _dot(f, wdown_ref[cols, :])
        val = val_next

    x2 = x1[main] + gate2 * acc
    ms = jnp.mean(x2 * x2, axis=-1, keepdims=True)
    out_ref[0] = x2 * lax.rsqrt(ms + EPS) * fg_ref[...]


def _ffn(x, y_conv, o_f, o_b, og, mod, gn, n2, fg, w_out, w_up, dw, dwb, w_down):
    bsz, length, d = x.shape
    tm = FFN_TILE
    per = tm // HALO
    n_halo = length // HALO

    def trio(w):
        return [pl.BlockSpec((1, HALO, w), lambda b, i: (b, jnp.maximum(i * per - 1, 0), 0)),
                pl.BlockSpec((1, tm, w), lambda b, i: (b, i, 0)),
                pl.BlockSpec((1, HALO, w), lambda b, i: (b, jnp.minimum((i + 1) * per, n_halo - 1), 0))]

    def const2(shape):
        return pl.BlockSpec(shape, lambda b, i: (0, 0))

    in_specs = (trio(d) + trio(CONV_W) + trio(GLA_DV) + trio(GLA_DV) + trio(GLA_DV)
                + [pl.BlockSpec((1, 1, N_MOD * d), lambda b, i: (b, 0, 0)),
                   const2((1, GLA_DV)), const2((1, d)), const2((1, d)),
                   const2((d, d)), const2((d, 2 * FFN_HIDDEN)),
                   const2((SUBLANES, FFN_HIDDEN)), const2((1, FFN_HIDDEN)),
                   const2((FFN_HIDDEN, d))])
    return pl.pallas_call(
        _ffn_kernel,
        grid=(bsz, length // tm),
        in_specs=in_specs,
        out_specs=pl.BlockSpec((1, tm, d), lambda b, i: (b, i, 0)),
        out_shape=jax.ShapeDtypeStruct((bsz, length, d), F32),
        scratch_shapes=[pltpu.VMEM((2, tm + 2 * HALO, FFN_CHUNK), F32)],
        compiler_params=pltpu.CompilerParams(dimension_semantics=("arbitrary", "arbitrary"),
                                             vmem_limit_bytes=VMEM_LIMIT),
        name="out_proj_ffn",
    )(x, x, x, y_conv, y_conv, y_conv, o_f, o_f, o_f, o_b, o_b, o_b, og, og, og,
      mod, gn, n2, fg, w_out, w_up, dw, dwb, w_down)


def kernel(x, c, ctx, c_ctx, w_mod, b_mod, norm1_g, w_in, conv_dw, conv_b, conv_ln_g, conv_ln_b,
           w_gf, b_gf, w_gb, b_gb, gla_norm_g, w_out, norm2_g, w_up, ffn_dw, ffn_dw_b, w_down, final_g):
    bsz, seq, d = x.shape
    ctx_len = ctx.shape[1]
    layer = 0

    cvec = jnp.concatenate([c, c_ctx[None, :], jnp.zeros((SUBLANES - bsz - 1, d), F32)], axis=0)
    mods = _modulation(cvec, w_mod[layer], b_mod[layer][None, :])
    mod_lat = mods[:bsz][:, None, :]
    mod_ctx = jnp.broadcast_to(mods[bsz][None, None, :], (bsz, 1, N_MOD * d))

    w_in_p = jnp.pad(w_in[layer], ((0, 0), (0, D_IN_PAD - w_in.shape[2]))).astype(BF16)
    zeros_g = jnp.zeros((GATE_RANK, GLA_DK), F32)
    wgf = jnp.concatenate([w_gf[layer], zeros_g], axis=0).astype(BF16)
    wgb = jnp.concatenate([zeros_g, w_gb[layer]], axis=0).astype(BF16)
    bgf = b_gf[layer][None, :]
    bgb = b_gb[layer][None, :]
    n1 = norm1_g[layer][None, :]

    _, _, k_c, v_c, _, z_c = _input_proj(ctx, mod_ctx, n1, w_in_p, ctx_len)
    zero_state = jnp.zeros((bsz, GLA_HEADS, GLA_HEAD_DK, GLA_HEAD_DV), F32)
    _, _, s_f, s_b = _gla(k_c, k_c, v_c, z_c, wgf, bgf, wgb, bgb, zero_state, zero_state)

    glu, q, k, v, og, z = _input_proj(x, mod_lat, n1, w_in_p, IN_TILE)
    o_f, o_b, _, _ = _gla(q, k, v, z, wgf, bgf, wgb, bgb, s_f, s_b)

    dw = jnp.pad(conv_dw[layer], ((0, 4 * SUBLANES - CONV_K), (0, 0)))
    y_conv = _conv_module(glu.reshape(bsz, seq // GRID_W, GRID_W, CONV_W),
                          dw[:, :CONV_HALF], dw[:, CONV_HALF:], conv_b[layer][None, :],
                          conv_ln_g[layer][None, :], conv_ln_b[layer][None, :])
    y_conv = y_conv.reshape(bsz, seq, CONV_W)

    ffn_w = jnp.pad(ffn_dw[layer], ((0, SUBLANES - ffn_dw.shape[1]), (0, 0)))
    return _ffn(x, y_conv, o_f, o_b, og, mod_lat, gla_norm_g[layer][None, :], norm2_g[layer][None, :],
                      final_g[None, :], w_out[layer].astype(BF16), w_up[layer].astype(BF16),
                      ffn_w, ffn_dw_b[layer][None, :], w_down[layer].astype(BF16))
```

```python
import jax
import jax.numpy as jnp
from jax import lax
from jax.experimental import pallas as pl
from jax.experimental.pallas import tpu as pltpu

F32 = jnp.float32
BF16 = jnp.bfloat16

D_MODEL = 1024
GRID_W = 64
CONV_W = 512
CONV_HALF = CONV_W // 2
CONV_K = 31
CONV_PAD = CONV_K // 2
GLA_HEADS = 4
GLA_DV = 512
GLA_HEAD_DV = 128
GLA_DK = 256
GLA_HEAD_DK = 64
GATE_RANK = 16
GATE_NORM = 16.0
CHUNK = 64
FFN_HIDDEN = 2816
N_MOD = 6
EPS = 1e-6

LANES = 128
SUBLANES = 8
V7X_VMEM_BYTES = 64 * 1024 * 1024
VMEM_LIMIT = V7X_VMEM_BYTES - 8 * 1024 * 1024

Z_PAD = 2 * GATE_RANK
D_IN_PAD = 2 * CONV_W + 2 * GLA_DK + 2 * GLA_DV + Z_PAD
OFF_CU, OFF_CG = 0, CONV_W
OFF_Q = 2 * CONV_W
OFF_K = OFF_Q + GLA_DK
OFF_V = OFF_K + GLA_DK
OFF_OG = OFF_V + GLA_DV
OFF_Z = OFF_OG + GLA_DV

IN_TILE = 1024
IN_SUB = 256
GLA_BLOCK = 1024
CONV_ROWS = 16
FFN_TILE = 512
FFN_CHUNK = 256
HALO = SUBLANES
UP_SLOTS = 3


def _sigmoid(x):
    return 1.0 / (1.0 + jnp.exp(-x))


def _split_bf16(x):
    hi = x.astype(BF16)
    lo = (x - hi.astype(F32)).astype(BF16)
    return hi, lo


def _dot(a, b):
    return jnp.dot(a, b, preferred_element_type=F32)


def _mod_kernel(c_ref, w_ref, b_ref, o_ref):
    c = c_ref[...]
    s = c * _sigmoid(c)
    s_hi, s_lo = _split_bf16(s)
    w_hi, w_lo = _split_bf16(w_ref[...])
    o_ref[...] = _dot(s_hi, w_hi) + _dot(s_lo, w_hi) + _dot(s_hi, w_lo) + b_ref[...]


def _modulation(cvec, w_mod, b_mod):
    rows, d = cvec.shape
    n = w_mod.shape[1]
    bn = D_MODEL
    return pl.pallas_call(
        _mod_kernel,
        grid=(n // bn,),
        in_specs=[pl.BlockSpec((rows, d), lambda j: (0, 0)),
                  pl.BlockSpec((d, bn), lambda j: (0, j)),
                  pl.BlockSpec((1, bn), lambda j: (0, j))],
        out_specs=pl.BlockSpec((rows, bn), lambda j: (0, j)),
        out_shape=jax.ShapeDtypeStruct((rows, n), F32),
        compiler_params=pltpu.CompilerParams(dimension_semantics=("arbitrary",),
                                             vmem_limit_bytes=VMEM_LIMIT),
        name="modulation",
    )(cvec, w_mod, b_mod)


def _in_kernel(x_ref, mod_ref, g_ref, w_ref, glu_ref, q_ref, k_ref, v_ref, og_ref, z_ref):
    m = mod_ref[0]
    shift = m[:, 0:D_MODEL]
    scale = m[:, D_MODEL:2 * D_MODEL]
    tm = x_ref.shape[1]
    sub = min(tm, IN_SUB)
    n_sub = tm // sub

    hs = []
    for s in range(n_sub):
        x = x_ref[0, s * sub:(s + 1) * sub, :]
        ms = jnp.mean(x * x, axis=-1, keepdims=True)
        y = x * lax.rsqrt(ms + EPS) * g_ref[...]
        hs.append((y * (1.0 + scale) + shift).astype(BF16))

    for s in range(n_sub):
        rows = slice(s * sub, (s + 1) * sub)

        def proj(lo, width):
            return _dot(hs[s], w_ref[:, lo:lo + width])

        glu_ref[0, rows, :] = (proj(OFF_CU, CONV_W) * _sigmoid(proj(OFF_CG, CONV_W))).astype(glu_ref.dtype)
        q_ref[0, rows, :] = proj(OFF_Q, GLA_DK).astype(q_ref.dtype)
        k_ref[0, rows, :] = proj(OFF_K, GLA_DK).astype(k_ref.dtype)
        v_ref[0, rows, :] = proj(OFF_V, GLA_DV).astype(v_ref.dtype)
        og_ref[0, rows, :] = proj(OFF_OG, GLA_DV).astype(og_ref.dtype)
        z_ref[0, rows, :] = proj(OFF_Z, Z_PAD).astype(z_ref.dtype)


def _input_proj(x, mod, norm_g, w_in_p, tm):
    bsz, length, d = x.shape
    widths = (CONV_W, GLA_DK, GLA_DK, GLA_DV, GLA_DV, Z_PAD)
    dtypes = (BF16, BF16, BF16, BF16, F32, BF16)
    return pl.pallas_call(
        _in_kernel,
        grid=(bsz, length // tm),
        in_specs=[pl.BlockSpec((1, tm, d), lambda b, i: (b, i, 0)),
                  pl.BlockSpec((1, 1, N_MOD * d), lambda b, i: (b, 0, 0)),
                  pl.BlockSpec((1, d), lambda b, i: (0, 0)),
                  pl.BlockSpec((d, D_IN_PAD), lambda b, i: (0, 0))],
        out_specs=[pl.BlockSpec((1, tm, w), lambda b, i: (b, i, 0)) for w in widths],
        out_shape=[jax.ShapeDtypeStruct((bsz, length, w), dt) for w, dt in zip(widths, dtypes)],
        compiler_params=pltpu.CompilerParams(dimension_semantics=("arbitrary", "arbitrary"),
                                             vmem_limit_bytes=VMEM_LIMIT),
        name="input_proj",
    )(x, mod, norm_g, w_in_p)


def _gla_block(q_ref, k_ref, v_ref, z_ref, wg, bg, s_ref, o_ref, reverse):
    n_chunks = q_ref.shape[1] // CHUNK
    row = lax.broadcasted_iota(jnp.int32, (CHUNK, CHUNK), 0)
    col = lax.broadcasted_iota(jnp.int32, (CHUNK, CHUNK), 1)
    tri = ((col >= row) if reverse else (col <= row)).astype(BF16)
    row2 = lax.broadcasted_iota(jnp.int32, (CHUNK, LANES), 0)
    lane = lax.broadcasted_iota(jnp.int32, (CHUNK, LANES), 1)
    col2 = lane & (CHUNK - 1)
    keep_pair = (col2 >= row2) if reverse else (col2 <= row2)
    first = lane < GLA_HEAD_DK
    zero = jnp.zeros((CHUNK, LANES), BF16)

    pre = _dot(z_ref[0, :, 0:2 * GATE_RANK].astype(BF16), wg) + bg
    g = (jnp.minimum(pre, 0.0) - jnp.log1p(jnp.exp(-jnp.abs(pre)))) * (1.0 / GATE_NORM)
    g_hi, g_lo = _split_bf16(g)
    b_parts, last_parts, decays = [], [], []
    for c in range(n_chunks):
        rows = slice(c * CHUNK, (c + 1) * CHUNK)
        b_c = _dot(tri, g_hi[rows]) + _dot(tri, g_lo[rows])
        b_last = b_c[0:1] if reverse else b_c[CHUNK - 1:CHUNK]
        b_parts.append(b_c)
        last_parts.append(jnp.broadcast_to(b_last, (CHUNK, GLA_DK)))
        decays.append(jnp.broadcast_to(jnp.exp(b_last), (LANES, GLA_DK)).T)
    b = jnp.concatenate(b_parts, axis=0)
    k = k_ref[0].astype(F32)
    q_e = (q_ref[0].astype(F32) * (GLA_HEAD_DK ** -0.5) * jnp.exp(b)).astype(BF16)
    k_e = k * jnp.exp(-b)
    k_t = (k * jnp.exp(jnp.concatenate(last_parts, axis=0) - b)).astype(BF16)
    v = v_ref[0].astype(BF16)

    o_intra, kvs = [], []
    for c in range(n_chunks):
        rows = slice(c * CHUNK, (c + 1) * CHUNK)
        o_c, kv_c = [], []
        for p in range(GLA_HEADS // 2):
            lanes = slice(p * LANES, (p + 1) * LANES)
            wide = slice(2 * p * LANES, 2 * (p + 1) * LANES)
            kg = k_e[rows, lanes]
            k_blk = jnp.concatenate([jnp.where(first, kg, 0.0), jnp.where(first, 0.0, kg)],
                                    axis=0).astype(BF16)
            scores = lax.dot_general(q_e[rows, lanes], k_blk, (((1,), (1,)), ((), ())),
                                     preferred_element_type=F32)
            a = jnp.where(keep_pair, scores, 0.0).astype(BF16)
            vp = v[rows, wide]
            v_blk = jnp.concatenate(
                [jnp.concatenate([vp[:, :LANES], zero], axis=1),
                 jnp.concatenate([zero, vp[:, LANES:]], axis=1)], axis=0)
            o_c.append((a, v_blk))
            kv_c.append(lax.dot_general(k_t[rows, lanes], vp,
                                        (((0,), (0,)), ((), ())), preferred_element_type=F32))
        o_intra.append(o_c)
        kvs.append(kv_c)

    state = [s_ref[h] for h in range(GLA_HEADS)]
    zero_s = jnp.zeros((GLA_HEAD_DK, GLA_HEAD_DV), F32)
    h0 = slice(0, GLA_HEAD_DK)
    h1 = slice(GLA_HEAD_DK, 2 * GLA_HEAD_DK)
    for c in (range(n_chunks - 1, -1, -1) if reverse else range(n_chunks)):
        rows = slice(c * CHUNK, (c + 1) * CHUNK)
        for p in range(GLA_HEADS // 2):
            lanes = slice(p * LANES, (p + 1) * LANES)
            wide = slice(2 * p * LANES, 2 * (p + 1) * LANES)
            s_blk = jnp.concatenate(
                [jnp.concatenate([state[2 * p], zero_s], axis=1),
                 jnp.concatenate([zero_s, state[2 * p + 1]], axis=1)], axis=0).astype(BF16)
            a, v_blk = o_intra[c][p]
            o_ref[0, rows, wide] = _dot(jnp.concatenate([q_e[rows, lanes], a], axis=1),
                                        jnp.concatenate([s_blk, v_blk], axis=0)).astype(o_ref.dtype)
            dp = decays[c][lanes]
            kv = kvs[c][p]
            state[2 * p] = dp[h0] * state[2 * p] + kv[h0, 0:LANES]
            state[2 * p + 1] = dp[h1] * state[2 * p + 1] + kv[h1, LANES:2 * LANES]
    for h in range(GLA_HEADS):
        s_ref[h] = state[h]


def _gla_kernel(qf_ref, kf_ref, vf_ref, zf_ref, qb_ref, kb_ref, vb_ref, zb_ref,
                wgf_ref, bgf_ref, wgb_ref, bgb_ref, sf0_ref, sb0_ref,
                of_ref, ob_ref, sf_out_ref, sb_out_ref, sf_ref, sb_ref):
    i = pl.program_id(1)

    @pl.when(i == 0)
    def _():
        sf_ref[...] = sf0_ref[0]
        sb_ref[...] = sb0_ref[0]

    _gla_block(qf_ref, kf_ref, vf_ref, zf_ref, wgf_ref[...], bgf_ref[...], sf_ref, of_ref, False)
    _gla_block(qb_ref, kb_ref, vb_ref, zb_ref, wgb_ref[...], bgb_ref[...], sb_ref, ob_ref, True)

    @pl.when(i == pl.num_programs(1) - 1)
    def _():
        sf_out_ref[0] = sf_ref[...]
        sb_out_ref[0] = sb_ref[...]


def _gla(q, k, v, z, wgf, bgf, wgb, bgb, sf0, sb0):
    bsz, length, _ = q.shape
    tb = min(GLA_BLOCK, length)
    assert length % tb == 0 and tb % CHUNK == 0
    nb = length // tb

    def fwd(w):
        return pl.BlockSpec((1, tb, w), lambda b, i: (b, i, 0))

    def bwd(w):
        return pl.BlockSpec((1, tb, w), lambda b, i: (b, nb - 1 - i, 0))

    def const2(shape):
        return pl.BlockSpec(shape, lambda b, i: (0, 0))

    state = pl.BlockSpec((1, GLA_HEADS, GLA_HEAD_DK, GLA_HEAD_DV), lambda b, i: (b, 0, 0, 0))
    state_shape = jax.ShapeDtypeStruct((bsz, GLA_HEADS, GLA_HEAD_DK, GLA_HEAD_DV), F32)
    o_shape = jax.ShapeDtypeStruct((bsz, length, GLA_DV), F32)
    return pl.pallas_call(
        _gla_kernel,
        grid=(bsz, nb),
        in_specs=[fwd(GLA_DK), fwd(GLA_DK), fwd(GLA_DV), fwd(Z_PAD),
                  bwd(GLA_DK), bwd(GLA_DK), bwd(GLA_DV), bwd(Z_PAD),
                  const2((2 * GATE_RANK, GLA_DK)), const2((1, GLA_DK)),
                  const2((2 * GATE_RANK, GLA_DK)), const2((1, GLA_DK)),
                  state, state],
        out_specs=[fwd(GLA_DV), bwd(GLA_DV), state, state],
        out_shape=[o_shape, o_shape, state_shape, state_shape],
        scratch_shapes=[pltpu.VMEM((GLA_HEADS, GLA_HEAD_DK, GLA_HEAD_DV), F32),
                        pltpu.VMEM((GLA_HEADS, GLA_HEAD_DK, GLA_HEAD_DV), F32)],
        compiler_params=pltpu.CompilerParams(dimension_semantics=("arbitrary", "arbitrary"),
                                             vmem_limit_bytes=VMEM_LIMIT),
        name="gla",
    )(q, k, v, z, q, k, v, z, wgf, bgf, wgb, bgb, sf0, sb0)


def _conv_kernel(grow_ref, gcol_ref, wrow_ref, wcol_ref, cb_ref, lg_ref, lb_ref, y_ref,
                 rowpad_ref, colpad_ref, yrow_ref):
    t = pl.program_id(1)
    n_rows = gcol_ref.shape[1]
    pad_rows = CONV_ROWS
    pad_w = 2 * SUBLANES

    @pl.when(t == 0)
    def _():
        rowpad_ref[...] = jnp.zeros(rowpad_ref.shape, F32)
        zeros = jnp.zeros((pad_rows, GRID_W, CONV_HALF), F32)
        colpad_ref[0:pad_rows] = zeros
        colpad_ref[pad_rows + n_rows:pad_rows + n_rows + pad_rows] = zeros

        def copy(j, carry):
            colpad_ref[pl.ds(pad_rows + j * CONV_ROWS, CONV_ROWS)] = (
                gcol_ref[0, pl.ds(j * CONV_ROWS, CONV_ROWS)].astype(F32))
            return carry
        lax.fori_loop(0, n_rows // CONV_ROWS, copy, 0)

    for lg in range(CONV_HALF // LANES):
        rowpad_ref[lg, :, pad_w:pad_w + GRID_W, :] = grow_ref[0, :, :, lg * LANES:(lg + 1) * LANES].astype(F32)

    half_w = GRID_W // 2

    def row_body(r, carry):
        pieces = []
        for lg in range(CONV_HALF // LANES):
            lanes = slice(lg * LANES, (lg + 1) * LANES)
            for parity in range(2):
                acc = jnp.broadcast_to(cb_ref[:, lanes], (half_w, LANES))
                for kk in range(CONV_K):
                    start = pad_w + kk - CONV_PAD + parity
                    acc = acc + (wrow_ref[kk:kk + 1, lanes]
                                 * rowpad_ref[lg, r, pl.ds(start, half_w, stride=2), :])
                yrow_ref[lg, pl.ds(parity, half_w, stride=2), :] = acc
            pieces.append(yrow_ref[lg])
        for lg in range(CONV_HALF // LANES):
            lanes = slice(lg * LANES, (lg + 1) * LANES)
            acc = jnp.broadcast_to(cb_ref[:, CONV_HALF + lg * LANES:CONV_HALF + (lg + 1) * LANES],
                                   (GRID_W, LANES))
            for kk in range(CONV_K):
                src = t * CONV_ROWS + r + (pad_rows + kk - CONV_PAD)
                acc = acc + wcol_ref[kk:kk + 1, lanes] * colpad_ref[src, :, lanes]
            pieces.append(acc)
        y = jnp.concatenate(pieces, axis=1)
        mu = jnp.mean(y, axis=-1, keepdims=True)
        var = jnp.mean(y * y, axis=-1, keepdims=True) - mu * mu
        yn = (y - mu) * lax.rsqrt(var + EPS) * lg_ref[...] + lb_ref[...]
        y_ref[0, r] = (yn * _sigmoid(yn)).astype(y_ref.dtype)
        return carry

    lax.fori_loop(0, CONV_ROWS, row_body, 0, unroll=2)


def _conv_module(glu4, w_row, w_col, conv_b, ln_g, ln_b):
    bsz, n_rows, gw, _ = glu4.shape
    return pl.pallas_call(
        _conv_kernel,
        grid=(bsz, n_rows // CONV_ROWS),
        in_specs=[pl.BlockSpec((1, CONV_ROWS, gw, CONV_HALF), lambda b, t: (b, t, 0, 0)),
                  pl.BlockSpec((1, n_rows, gw, CONV_HALF), lambda b, t: (b, 0, 0, 1)),
                  pl.BlockSpec((4 * SUBLANES, CONV_HALF), lambda b, t: (0, 0)),
                  pl.BlockSpec((4 * SUBLANES, CONV_HALF), lambda b, t: (0, 0)),
                  pl.BlockSpec((1, CONV_W), lambda b, t: (0, 0)),
                  pl.BlockSpec((1, CONV_W), lambda b, t: (0, 0)),
                  pl.BlockSpec((1, CONV_W), lambda b, t: (0, 0))],
        out_specs=pl.BlockSpec((1, CONV_ROWS, gw, CONV_W), lambda b, t: (b, t, 0, 0)),
        out_shape=jax.ShapeDtypeStruct((bsz, n_rows, gw, CONV_W), F32),
        scratch_shapes=[pltpu.VMEM((CONV_HALF // LANES, CONV_ROWS, gw + 4 * SUBLANES, LANES), F32),
                        pltpu.VMEM((n_rows + 2 * CONV_ROWS, gw, CONV_HALF), F32),
                        pltpu.VMEM((CONV_HALF // LANES, gw, LANES), F32)],
        compiler_params=pltpu.CompilerParams(dimension_semantics=("arbitrary", "arbitrary"),
                                             vmem_limit_bytes=VMEM_LIMIT),
        name="conv_module",
    )(glu4, glu4, w_row, w_col, conv_b, ln_g, ln_b)


def _ffn_kernel(x_p, x_m, x_n, yc_p, yc_m, yc_n, of_p, of_m, of_n, ob_p, ob_m, ob_n,
                      og_p, og_m, og_n, mod_ref, gn_ref, n2_ref, fg_ref, wout_ref, wup_ref,
                      dw_ref, dwb_ref, wdown_ref, out_ref, a_ref):
    i = pl.program_id(1)
    last = pl.num_programs(1) - 1
    tm = x_m.shape[1]
    main = slice(HALO, HALO + tm)

    def ext(p, m, n):
        return jnp.concatenate([p[0], m[0], n[0]], axis=0)

    mod = mod_ref[0]
    gate1 = mod[:, 2 * D_MODEL:3 * D_MODEL]
    shift2 = mod[:, 3 * D_MODEL:4 * D_MODEL]
    scale2 = mod[:, 4 * D_MODEL:5 * D_MODEL]
    gate2 = mod[:, 5 * D_MODEL:6 * D_MODEL]

    o = ext(of_p, of_m, of_n) + ext(ob_p, ob_m, ob_n)
    og = ext(og_p, og_m, og_n)
    parts = [ext(yc_p, yc_m, yc_n).astype(BF16)]
    for h in range(GLA_HEADS):
        lanes = slice(h * GLA_HEAD_DV, (h + 1) * GLA_HEAD_DV)
        oh = o[:, lanes]
        ms = jnp.mean(oh * oh, axis=-1, keepdims=True)
        ogh = og[:, lanes]
        parts.append((oh * lax.rsqrt(ms + EPS) * gn_ref[:, lanes] * (ogh * _sigmoid(ogh))).astype(BF16))
    mix = jnp.concatenate(parts, axis=1)
    x1 = ext(x_p, x_m, x_n) + gate1 * _dot(mix, wout_ref[...])

    ms = jnp.mean(x1 * x1, axis=-1, keepdims=True)
    h2 = ((x1 * lax.rsqrt(ms + EPS) * n2_ref[...]) * (1.0 + scale2) + shift2).astype(BF16)

    row = lax.broadcasted_iota(jnp.int32, (tm, 1), 0)
    kill_prev = jnp.logical_and(row == 0, i == 0)
    kill_next = jnp.logical_and(row == tm - 1, i == last)
    n_chunks = FFN_HIDDEN // FFN_CHUNK

    def up(j):
        a_ref[j % UP_SLOTS] = _dot(h2, wup_ref[:, j * FFN_CHUNK:(j + 1) * FFN_CHUNK])
        return _dot(h2, wup_ref[:, FFN_HIDDEN + j * FFN_CHUNK:FFN_HIDDEN + (j + 1) * FFN_CHUNK])[main]

    acc = jnp.zeros((tm, D_MODEL), F32)
    vals = {j: up(j) for j in range(UP_SLOTS - 1)}
    for j in range(n_chunks):
        if j + UP_SLOTS - 1 < n_chunks:
            vals[j + UP_SLOTS - 1] = up(j + UP_SLOTS - 1)
        val = vals.pop(j)
        cols = slice(j * FFN_CHUNK, (j + 1) * FFN_CHUNK)
        a_cur = a_ref.at[j % UP_SLOTS]
        a_prev = jnp.where(kill_prev, 0.0, a_cur[HALO - 1:HALO - 1 + tm, :])
        a_next = jnp.where(kill_next, 0.0, a_cur[HALO + 1:HALO + 1 + tm, :])
        ac = (dw_ref[0:1, cols] * a_prev + dw_ref[1:2, cols] * a_cur[main, :]
              + dw_ref[2:3, cols] * a_next + dwb_ref[:, cols])
        f = (ac * _sigmoid(ac) * val).astype(BF16)
        acc = acc + _dot(f, wdown_ref[cols, :])

    x2 = x1[main] + gate2 * acc
    ms = jnp.mean(x2 * x2, axis=-1, keepdims=True)
    out_ref[0] = x2 * lax.rsqrt(ms + EPS) * fg_ref[...]


def _ffn(x, y_conv, o_f, o_b, og, mod, gn, n2, fg, w_out, w_up, dw, dwb, w_down):
    bsz, length, d = x.shape
    tm = FFN_TILE
    per = tm // HALO
    n_halo = length // HALO

    def trio(w):
        return [pl.BlockSpec((1, HALO, w), lambda b, i: (b, jnp.maximum(i * per - 1, 0), 0)),
                pl.BlockSpec((1, tm, w), lambda b, i: (b, i, 0)),
                pl.BlockSpec((1, HALO, w), lambda b, i: (b, jnp.minimum((i + 1) * per, n_halo - 1), 0))]

    def const2(shape):
        return pl.BlockSpec(shape, lambda b, i: (0, 0))

    in_specs = (trio(d) + trio(CONV_W) + trio(GLA_DV) + trio(GLA_DV) + trio(GLA_DV)
                + [pl.BlockSpec((1, 1, N_MOD * d), lambda b, i: (b, 0, 0)),
                   const2((1, GLA_DV)), const2((1, d)), const2((1, d)),
                   const2((d, d)), const2((d, 2 * FFN_HIDDEN)),
                   const2((SUBLANES, FFN_HIDDEN)), const2((1, FFN_HIDDEN)),
                   const2((FFN_HIDDEN, d))])
    return pl.pallas_call(
        _ffn_kernel,
        grid=(bsz, length // tm),
        in_specs=in_specs,
        out_specs=pl.BlockSpec((1, tm, d), lambda b, i: (b, i, 0)),
        out_shape=jax.ShapeDtypeStruct((bsz, length, d), F32),
        scratch_shapes=[pltpu.VMEM((UP_SLOTS, tm + 2 * HALO, FFN_CHUNK), F32)],
        compiler_params=pltpu.CompilerParams(dimension_semantics=("arbitrary", "arbitrary"),
                                             vmem_limit_bytes=VMEM_LIMIT),
        name="out_proj_ffn",
    )(x, x, x, y_conv, y_conv, y_conv, o_f, o_f, o_f, o_b, o_b, o_b, og, og, og,
      mod, gn, n2, fg, w_out, w_up, dw, dwb, w_down)


def kernel(x, c, ctx, c_ctx, w_mod, b_mod, norm1_g, w_in, conv_dw, conv_b, conv_ln_g, conv_ln_b,
           w_gf, b_gf, w_gb, b_gb, gla_norm_g, w_out, norm2_g, w_up, ffn_dw, ffn_dw_b, w_down, final_g):
    bsz, seq, d = x.shape
    ctx_len = ctx.shape[1]
    layer = 0

    cvec = jnp.concatenate([c, c_ctx[None, :], jnp.zeros((SUBLANES - bsz - 1, d), F32)], axis=0)
    mods = _modulation(cvec, w_mod[layer], b_mod[layer][None, :])
    mod_lat = mods[:bsz][:, None, :]
    mod_ctx = jnp.broadcast_to(mods[bsz][None, None, :], (bsz, 1, N_MOD * d))

    assert w_in.shape[2] == D_IN_PAD
    w_in_p = w_in[layer]
    zeros_g = jnp.zeros((GATE_RANK, GLA_DK), F32)
    wgf = jnp.concatenate([w_gf[layer], zeros_g], axis=0).astype(BF16)
    wgb = jnp.concatenate([zeros_g, w_gb[layer]], axis=0).astype(BF16)
    bgf = b_gf[layer][None, :]
    bgb = b_gb[layer][None, :]
    n1 = norm1_g[layer][None, :]

    _, _, k_c, v_c, _, z_c = _input_proj(ctx, mod_ctx, n1, w_in_p, ctx_len)
    zero_state = jnp.zeros((bsz, GLA_HEADS, GLA_HEAD_DK, GLA_HEAD_DV), F32)
    _, _, s_f, s_b = _gla(k_c, k_c, v_c, z_c, wgf, bgf, wgb, bgb, zero_state, zero_state)

    glu, q, k, v, og, z = _input_proj(x, mod_lat, n1, w_in_p, IN_TILE)
    o_f, o_b, _, _ = _gla(q, k, v, z, wgf, bgf, wgb, bgb, s_f, s_b)

    dw = jnp.pad(conv_dw[layer], ((0, 4 * SUBLANES - CONV_K), (0, 0)))
    y_conv = _conv_module(glu.reshape(bsz, seq // GRID_W, GRID_W, CONV_W),
                          dw[:, :CONV_HALF], dw[:, CONV_HALF:], conv_b[layer][None, :],
                          conv_ln_g[layer][None, :], conv_ln_b[layer][None, :])
    y_conv = y_conv.reshape(bsz, seq, CONV_W)

    ffn_w = jnp.pad(ffn_dw[layer], ((0, SUBLANES - ffn_dw.shape[1]), (0, 0)))
    return _ffn(x, y_conv, o_f, o_b, og, mod_lat, gla_norm_g[layer][None, :], norm2_g[layer][None, :],
                      final_g[None, :], w_out[layer].astype(BF16), w_up[layer].astype(BF16),
                      ffn_w, ffn_dw_b[layer][None, :], w_down[layer].astype(BF16))
```

```python
import functools

import jax
import jax.numpy as jnp
from jax import lax
from jax.experimental import pallas as pl
from jax.experimental.pallas import tpu as pltpu

F32 = jnp.float32
BF16 = jnp.bfloat16

D_MODEL = 1024
GRID_W = 64
CONV_W = 512
CONV_HALF = CONV_W // 2
CONV_K = 31
CONV_PAD = CONV_K // 2
GLA_HEADS = 4
GLA_DV = 512
GLA_HEAD_DV = 128
GLA_DK = 256
GLA_HEAD_DK = 64
GATE_RANK = 16
GATE_NORM = 16.0
CHUNK = 64
FFN_HIDDEN = 2816
N_MOD = 6
EPS = 1e-6

LANES = 128
SUBLANES = 8
V7X_VMEM_BYTES = 64 * 1024 * 1024
VMEM_LIMIT = V7X_VMEM_BYTES - 8 * 1024 * 1024

Z_PAD = 2 * GATE_RANK
D_IN_PAD = 2 * CONV_W + 2 * GLA_DK + 2 * GLA_DV + Z_PAD
OFF_CU, OFF_CG = 0, CONV_W
OFF_Q = 2 * CONV_W
OFF_K = OFF_Q + GLA_DK
OFF_V = OFF_K + GLA_DK
OFF_OG = OFF_V + GLA_DV
OFF_Z = OFF_OG + GLA_DV

IN_TILE = 1024
IN_SUB = 256
GLA_BLOCK = 1024
CONV_ROWS = 16
FFN_TILE = 512
FFN_CHUNK = 256
HALO = SUBLANES
UP_SLOTS = 3


def _sigmoid(x):
    return 1.0 / (1.0 + jnp.exp(-x))


def _split_bf16(x):
    hi = x.astype(BF16)
    lo = (x - hi.astype(F32)).astype(BF16)
    return hi, lo


def _dot(a, b):
    return jnp.dot(a, b, preferred_element_type=F32)


def _mod_kernel(c_ref, w_ref, b_ref, o_ref):
    c = c_ref[...]
    s = c * _sigmoid(c)
    s_hi, s_lo = _split_bf16(s)
    w = w_ref[...].astype(BF16)
    o_ref[...] = _dot(s_hi, w) + _dot(s_lo, w) + b_ref[...]


def _modulation(cvec, w_mod, b_mod):
    rows, d = cvec.shape
    n = w_mod.shape[1]
    bn = D_MODEL
    return pl.pallas_call(
        _mod_kernel,
        grid=(n // bn,),
        in_specs=[pl.BlockSpec((rows, d), lambda j: (0, 0)),
                  pl.BlockSpec((d, bn), lambda j: (0, j)),
                  pl.BlockSpec((1, bn), lambda j: (0, j))],
        out_specs=pl.BlockSpec((rows, bn), lambda j: (0, j)),
        out_shape=jax.ShapeDtypeStruct((rows, n), F32),
        compiler_params=pltpu.CompilerParams(dimension_semantics=("arbitrary",),
                                             vmem_limit_bytes=VMEM_LIMIT),
        name="modulation",
    )(cvec, w_mod, b_mod)


def _in_kernel(x_ref, mod_ref, g_ref, w_ref, *out_refs, full):
    m = mod_ref[0]
    shift = m[:, 0:D_MODEL]
    scale = m[:, D_MODEL:2 * D_MODEL]
    tm = x_ref.shape[1]
    sub = min(tm, IN_SUB)
    n_sub = tm // sub

    hs = []
    for s in range(n_sub):
        x = x_ref[0, s * sub:(s + 1) * sub, :]
        ms = jnp.mean(x * x, axis=-1, keepdims=True)
        y = x * lax.rsqrt(ms + EPS) * g_ref[...]
        hs.append((y * (1.0 + scale) + shift).astype(BF16))

    for s in range(n_sub):
        rows = slice(s * sub, (s + 1) * sub)

        def proj(lo, width):
            return _dot(hs[s], w_ref[:, lo:lo + width])

        if full:
            glu_ref, q_ref, k_ref, v_ref, og_ref, z_ref = out_refs
            glu_ref[0, rows, :] = (proj(OFF_CU, CONV_W) * _sigmoid(proj(OFF_CG, CONV_W))).astype(glu_ref.dtype)
            q_ref[0, rows, :] = proj(OFF_Q, GLA_DK).astype(q_ref.dtype)
            og_ref[0, rows, :] = proj(OFF_OG, GLA_DV).astype(og_ref.dtype)
        else:
            k_ref, v_ref, z_ref = out_refs
        k_ref[0, rows, :] = proj(OFF_K, GLA_DK).astype(k_ref.dtype)
        v_ref[0, rows, :] = proj(OFF_V, GLA_DV).astype(v_ref.dtype)
        z_ref[0, rows, :] = proj(OFF_Z, Z_PAD).astype(z_ref.dtype)


def _input_proj(x, mod, norm_g, w_in_p, tm, full=True):
    bsz, length, d = x.shape
    if full:
        widths = (CONV_W, GLA_DK, GLA_DK, GLA_DV, GLA_DV, Z_PAD)
        dtypes = (BF16, BF16, BF16, BF16, F32, BF16)
    else:
        widths = (GLA_DK, GLA_DV, Z_PAD)
        dtypes = (BF16, BF16, BF16)
    return pl.pallas_call(
        functools.partial(_in_kernel, full=full),
        grid=(bsz, length // tm),
        in_specs=[pl.BlockSpec((1, tm, d), lambda b, i: (b, i, 0)),
                  pl.BlockSpec((1, 1, N_MOD * d), lambda b, i: (b, 0, 0)),
                  pl.BlockSpec((1, d), lambda b, i: (0, 0)),
                  pl.BlockSpec((d, D_IN_PAD), lambda b, i: (0, 0))],
        out_specs=[pl.BlockSpec((1, tm, w), lambda b, i: (b, i, 0)) for w in widths],
        out_shape=[jax.ShapeDtypeStruct((bsz, length, w), dt) for w, dt in zip(widths, dtypes)],
        compiler_params=pltpu.CompilerParams(dimension_semantics=("arbitrary", "arbitrary"),
                                             vmem_limit_bytes=VMEM_LIMIT),
        name="input_proj",
    )(x, mod, norm_g, w_in_p)


def _gla_block(q_ref, k_ref, v_ref, z_ref, wg, bg, s_ref, o_ref, reverse):
    n_chunks = k_ref.shape[1] // CHUNK
    row = lax.broadcasted_iota(jnp.int32, (CHUNK, CHUNK), 0)
    col = lax.broadcasted_iota(jnp.int32, (CHUNK, CHUNK), 1)
    tri = ((col >= row) if reverse else (col <= row)).astype(BF16)
    row2 = lax.broadcasted_iota(jnp.int32, (CHUNK, LANES), 0)
    lane = lax.broadcasted_iota(jnp.int32, (CHUNK, LANES), 1)
    col2 = lane & (CHUNK - 1)
    keep_pair = (col2 >= row2) if reverse else (col2 <= row2)
    first = lane < GLA_HEAD_DK
    zero = jnp.zeros((CHUNK, LANES), BF16)

    pre = _dot(z_ref[0, :, 0:2 * GATE_RANK].astype(BF16), wg) + bg
    g = (jnp.minimum(pre, 0.0) - jnp.log1p(jnp.exp(-jnp.abs(pre)))) * (1.0 / GATE_NORM)
    g_hi, g_lo = _split_bf16(g)
    b_parts, last_parts, decays = [], [], []
    for c in range(n_chunks):
        rows = slice(c * CHUNK, (c + 1) * CHUNK)
        b_c = _dot(tri, g_hi[rows]) + _dot(tri, g_lo[rows])
        b_last = b_c[0:1] if reverse else b_c[CHUNK - 1:CHUNK]
        b_parts.append(b_c)
        last_parts.append(jnp.broadcast_to(b_last, (CHUNK, GLA_DK)))
        decays.append(jnp.broadcast_to(jnp.exp(b_last), (LANES, GLA_DK)).T)
    b = jnp.concatenate(b_parts, axis=0)
    need_out = o_ref is not None
    k = k_ref[0].astype(F32)
    if need_out:
        q_e = (q_ref[0].astype(F32) * (GLA_HEAD_DK ** -0.5) * jnp.exp(b)).astype(BF16)
        k_e = k * jnp.exp(-b)
    k_t = (k * jnp.exp(jnp.concatenate(last_parts, axis=0) - b)).astype(BF16)
    v = v_ref[0].astype(BF16)

    o_intra, kvs = [], []
    for c in range(n_chunks):
        rows = slice(c * CHUNK, (c + 1) * CHUNK)
        o_c, kv_c = [], []
        for p in range(GLA_HEADS // 2):
            lanes = slice(p * LANES, (p + 1) * LANES)
            wide = slice(2 * p * LANES, 2 * (p + 1) * LANES)
            vp = v[rows, wide]
            if need_out:
                kg = k_e[rows, lanes]
                k_blk = jnp.concatenate([jnp.where(first, kg, 0.0), jnp.where(first, 0.0, kg)],
                                        axis=0).astype(BF16)
                scores = lax.dot_general(q_e[rows, lanes], k_blk, (((1,), (1,)), ((), ())),
                                         preferred_element_type=F32)
                a = jnp.where(keep_pair, scores, 0.0).astype(BF16)
                v_blk = jnp.concatenate(
                    [jnp.concatenate([vp[:, :LANES], zero], axis=1),
                     jnp.concatenate([zero, vp[:, LANES:]], axis=1)], axis=0)
                o_c.append((a, v_blk))
            kv_c.append(lax.dot_general(k_t[rows, lanes], vp,
                                        (((0,), (0,)), ((), ())), preferred_element_type=F32))
        o_intra.append(o_c)
        kvs.append(kv_c)

    state = [s_ref[h] for h in range(GLA_HEADS)]
    zero_s = jnp.zeros((GLA_HEAD_DK, GLA_HEAD_DV), F32)
    h0 = slice(0, GLA_HEAD_DK)
    h1 = slice(GLA_HEAD_DK, 2 * GLA_HEAD_DK)
    for c in (range(n_chunks - 1, -1, -1) if reverse else range(n_chunks)):
        rows = slice(c * CHUNK, (c + 1) * CHUNK)
        for p in range(GLA_HEADS // 2):
            lanes = slice(p * LANES, (p + 1) * LANES)
            wide = slice(2 * p * LANES, 2 * (p + 1) * LANES)
            if need_out:
                s_blk = jnp.concatenate(
                    [jnp.concatenate([state[2 * p], zero_s], axis=1),
                     jnp.concatenate([zero_s, state[2 * p + 1]], axis=1)], axis=0).astype(BF16)
                a, v_blk = o_intra[c][p]
                o_ref[0, rows, wide] = _dot(jnp.concatenate([q_e[rows, lanes], a], axis=1),
                                            jnp.concatenate([s_blk, v_blk], axis=0)).astype(o_ref.dtype)
            dp = decays[c][lanes]
            kv = kvs[c][p]
            state[2 * p] = dp[h0] * state[2 * p] + kv[h0, 0:LANES]
            state[2 * p + 1] = dp[h1] * state[2 * p + 1] + kv[h1, LANES:2 * LANES]
    for h in range(GLA_HEADS):
        s_ref[h] = state[h]


def _gla_kernel(*refs, need_out):
    if need_out:
        (qf_ref, kf_ref, vf_ref, zf_ref, qb_ref, kb_ref, vb_ref, zb_ref,
         wgf_ref, bgf_ref, wgb_ref, bgb_ref, sf0_ref, sb0_ref,
         of_ref, ob_ref, sf_out_ref, sb_out_ref, sf_ref, sb_ref) = refs
    else:
        (kf_ref, vf_ref, zf_ref, kb_ref, vb_ref, zb_ref,
         wgf_ref, bgf_ref, wgb_ref, bgb_ref, sf0_ref, sb0_ref,
         sf_out_ref, sb_out_ref, sf_ref, sb_ref) = refs
        qf_ref = qb_ref = of_ref = ob_ref = None
    i = pl.program_id(1)

    @pl.when(i == 0)
    def _():
        sf_ref[...] = sf0_ref[0]
        sb_ref[...] = sb0_ref[0]

    _gla_block(qf_ref, kf_ref, vf_ref, zf_ref, wgf_ref[...], bgf_ref[...], sf_ref, of_ref, False)
    _gla_block(qb_ref, kb_ref, vb_ref, zb_ref, wgb_ref[...], bgb_ref[...], sb_ref, ob_ref, True)

    @pl.when(i == pl.num_programs(1) - 1)
    def _():
        sf_out_ref[0] = sf_ref[...]
        sb_out_ref[0] = sb_ref[...]


def _gla(q, k, v, z, wgf, bgf, wgb, bgb, sf0, sb0):
    need_out = q is not None
    bsz, length, _ = k.shape
    tb = min(GLA_BLOCK, length)
    assert length % tb == 0 and tb % CHUNK == 0
    nb = length // tb

    def fwd(w):
        return pl.BlockSpec((1, tb, w), lambda b, i: (b, i, 0))

    def bwd(w):
        return pl.BlockSpec((1, tb, w), lambda b, i: (b, nb - 1 - i, 0))

    def const2(shape):
        return pl.BlockSpec(shape, lambda b, i: (0, 0))

    state = pl.BlockSpec((1, GLA_HEADS, GLA_HEAD_DK, GLA_HEAD_DV), lambda b, i: (b, 0, 0, 0))
    state_shape = jax.ShapeDtypeStruct((bsz, GLA_HEADS, GLA_HEAD_DK, GLA_HEAD_DV), F32)
    o_shape = jax.ShapeDtypeStruct((bsz, length, GLA_DV), F32)
    q_f = [fwd(GLA_DK)] if need_out else []
    q_b = [bwd(GLA_DK)] if need_out else []
    q_arg = [q] if need_out else []
    return pl.pallas_call(
        functools.partial(_gla_kernel, need_out=need_out),
        grid=(bsz, nb),
        in_specs=(q_f + [fwd(GLA_DK), fwd(GLA_DV), fwd(Z_PAD)]
                  + q_b + [bwd(GLA_DK), bwd(GLA_DV), bwd(Z_PAD)]
                  + [const2((2 * GATE_RANK, GLA_DK)), const2((1, GLA_DK)),
                     const2((2 * GATE_RANK, GLA_DK)), const2((1, GLA_DK)),
                     state, state]),
        out_specs=([fwd(GLA_DV), bwd(GLA_DV)] if need_out else []) + [state, state],
        out_shape=([o_shape, o_shape] if need_out else []) + [state_shape, state_shape],
        scratch_shapes=[pltpu.VMEM((GLA_HEADS, GLA_HEAD_DK, GLA_HEAD_DV), F32),
                        pltpu.VMEM((GLA_HEADS, GLA_HEAD_DK, GLA_HEAD_DV), F32)],
        compiler_params=pltpu.CompilerParams(dimension_semantics=("arbitrary", "arbitrary"),
                                             vmem_limit_bytes=VMEM_LIMIT),
        name="gla",
    )(*q_arg, k, v, z, *q_arg, k, v, z, wgf, bgf, wgb, bgb, sf0, sb0)


def _conv_kernel(grow_ref, gcol_ref, wrow_ref, wcol_ref, cb_ref, lg_ref, lb_ref, y_ref,
                 rowpad_ref, colpad_ref, yrow_ref):
    t = pl.program_id(1)
    n_rows = gcol_ref.shape[1]
    pad_rows = CONV_ROWS
    pad_w = 2 * SUBLANES

    @pl.when(t == 0)
    def _():
        rowpad_ref[...] = jnp.zeros(rowpad_ref.shape, F32)
        zeros = jnp.zeros((pad_rows, GRID_W, CONV_HALF), F32)
        colpad_ref[0:pad_rows] = zeros
        colpad_ref[pad_rows + n_rows:pad_rows + n_rows + pad_rows] = zeros

        def copy(j, carry):
            colpad_ref[pl.ds(pad_rows + j * CONV_ROWS, CONV_ROWS)] = (
                gcol_ref[0, pl.ds(j * CONV_ROWS, CONV_ROWS)].astype(F32))
            return carry
        lax.fori_loop(0, n_rows // CONV_ROWS, copy, 0)

    for lg in range(CONV_HALF // LANES):
        rowpad_ref[lg, :, pad_w:pad_w + GRID_W, :] = grow_ref[0, :, :, lg * LANES:(lg + 1) * LANES].astype(F32)

    half_w = GRID_W // 2

    def row_body(r, carry):
        pieces = []
        for lg in range(CONV_HALF // LANES):
            lanes = slice(lg * LANES, (lg + 1) * LANES)
            for parity in range(2):
                acc = jnp.broadcast_to(cb_ref[:, lanes], (half_w, LANES))
                for kk in range(CONV_K):
                    start = pad_w + kk - CONV_PAD + parity
                    acc = acc + (wrow_ref[kk:kk + 1, lanes]
                                 * rowpad_ref[lg, r, pl.ds(start, half_w, stride=2), :])
                yrow_ref[lg, pl.ds(parity, half_w, stride=2), :] = acc
            pieces.append(yrow_ref[lg])
        for lg in range(CONV_HALF // LANES):
            lanes = slice(lg * LANES, (lg + 1) * LANES)
            acc = jnp.broadcast_to(cb_ref[:, CONV_HALF + lg * LANES:CONV_HALF + (lg + 1) * LANES],
                                   (GRID_W, LANES))
            for kk in range(CONV_K):
                src = t * CONV_ROWS + r + (pad_rows + kk - CONV_PAD)
                acc = acc + wcol_ref[kk:kk + 1, lanes] * colpad_ref[src, :, lanes]
            pieces.append(acc)
        y = jnp.concatenate(pieces, axis=1)
        mu = jnp.mean(y, axis=-1, keepdims=True)
        var = jnp.mean(y * y, axis=-1, keepdims=True) - mu * mu
        yn = (y - mu) * lax.rsqrt(var + EPS) * lg_ref[...] + lb_ref[...]
        y_ref[0, r] = (yn * _sigmoid(yn)).astype(y_ref.dtype)
        return carry

    lax.fori_loop(0, CONV_ROWS, row_body, 0, unroll=2)


def _conv_module(glu4, w_row, w_col, conv_b, ln_g, ln_b):
    bsz, n_rows, gw, _ = glu4.shape
    return pl.pallas_call(
        _conv_kernel,
        grid=(bsz, n_rows // CONV_ROWS),
        in_specs=[pl.BlockSpec((1, CONV_ROWS, gw, CONV_HALF), lambda b, t: (b, t, 0, 0)),
                  pl.BlockSpec((1, n_rows, gw, CONV_HALF), lambda b, t: (b, 0, 0, 1)),
                  pl.BlockSpec((4 * SUBLANES, CONV_HALF), lambda b, t: (0, 0)),
                  pl.BlockSpec((4 * SUBLANES, CONV_HALF), lambda b, t: (0, 0)),
                  pl.BlockSpec((1, CONV_W), lambda b, t: (0, 0)),
                  pl.BlockSpec((1, CONV_W), lambda b, t: (0, 0)),
                  pl.BlockSpec((1, CONV_W), lambda b, t: (0, 0))],
        out_specs=pl.BlockSpec((1, CONV_ROWS, gw, CONV_W), lambda b, t: (b, t, 0, 0)),
        out_shape=jax.ShapeDtypeStruct((bsz, n_rows, gw, CONV_W), F32),
        scratch_shapes=[pltpu.VMEM((CONV_HALF // LANES, CONV_ROWS, gw + 4 * SUBLANES, LANES), F32),
                        pltpu.VMEM((n_rows + 2 * CONV_ROWS, gw, CONV_HALF), F32),
                        pltpu.VMEM((CONV_HALF // LANES, gw, LANES), F32)],
        compiler_params=pltpu.CompilerParams(dimension_semantics=("arbitrary", "arbitrary"),
                                             vmem_limit_bytes=VMEM_LIMIT),
        name="conv_module",
    )(glu4, glu4, w_row, w_col, conv_b, ln_g, ln_b)


def _ffn_kernel(x_p, x_m, x_n, yc_p, yc_m, yc_n, of_p, of_m, of_n, ob_p, ob_m, ob_n,
                      og_p, og_m, og_n, mod_ref, gn_ref, n2_ref, fg_ref, wout_ref, wup_ref,
                      dw_ref, dwb_ref, wdown_ref, out_ref, a_ref):
    i = pl.program_id(1)
    last = pl.num_programs(1) - 1
    tm = x_m.shape[1]
    main = slice(HALO, HALO + tm)

    def ext(p, m, n):
        return jnp.concatenate([p[0], m[0], n[0]], axis=0)

    mod = mod_ref[0]
    gate1 = mod[:, 2 * D_MODEL:3 * D_MODEL]
    shift2 = mod[:, 3 * D_MODEL:4 * D_MODEL]
    scale2 = mod[:, 4 * D_MODEL:5 * D_MODEL]
    gate2 = mod[:, 5 * D_MODEL:6 * D_MODEL]

    o = ext(of_p, of_m, of_n) + ext(ob_p, ob_m, ob_n)
    og = ext(og_p, og_m, og_n)
    parts = [ext(yc_p, yc_m, yc_n).astype(BF16)]
    for h in range(GLA_HEADS):
        lanes = slice(h * GLA_HEAD_DV, (h + 1) * GLA_HEAD_DV)
        oh = o[:, lanes]
        ms = jnp.mean(oh * oh, axis=-1, keepdims=True)
        ogh = og[:, lanes]
        parts.append((oh * lax.rsqrt(ms + EPS) * gn_ref[:, lanes] * (ogh * _sigmoid(ogh))).astype(BF16))
    mix = jnp.concatenate(parts, axis=1)
    x1 = ext(x_p, x_m, x_n) + gate1 * _dot(mix, wout_ref[...])

    ms = jnp.mean(x1 * x1, axis=-1, keepdims=True)
    h2 = ((x1 * lax.rsqrt(ms + EPS) * n2_ref[...]) * (1.0 + scale2) + shift2).astype(BF16)

    row = lax.broadcasted_iota(jnp.int32, (tm, 1), 0)
    kill_prev = jnp.logical_and(row == 0, i == 0)
    kill_next = jnp.logical_and(row == tm - 1, i == last)
    n_chunks = FFN_HIDDEN // FFN_CHUNK

    def up(j):
        a_ref[j % UP_SLOTS] = _dot(h2, wup_ref[:, j * FFN_CHUNK:(j + 1) * FFN_CHUNK])
        return _dot(h2, wup_ref[:, FFN_HIDDEN + j * FFN_CHUNK:FFN_HIDDEN + (j + 1) * FFN_CHUNK])[main]

    acc = jnp.zeros((tm, D_MODEL), F32)
    vals = {j: up(j) for j in range(UP_SLOTS - 1)}
    pending = None
    for j in range(n_chunks):
        if j + UP_SLOTS - 1 < n_chunks:
            vals[j + UP_SLOTS - 1] = up(j + UP_SLOTS - 1)
        if pending is not None:
            acc = acc + _dot(pending[0], wdown_ref[pending[1], :])
        val = vals.pop(j)
        cols = slice(j * FFN_CHUNK, (j + 1) * FFN_CHUNK)
        a_cur = a_ref.at[j % UP_SLOTS]
        a_prev = jnp.where(kill_prev, 0.0, a_cur[HALO - 1:HALO - 1 + tm, :])
        a_next = jnp.where(kill_next, 0.0, a_cur[HALO + 1:HALO + 1 + tm, :])
        ac = (dw_ref[0:1, cols] * a_prev + dw_ref[1:2, cols] * a_cur[main, :]
              + dw_ref[2:3, cols] * a_next + dwb_ref[:, cols])
        f = (ac * _sigmoid(ac) * val).astype(BF16)
        pending = (f, cols)
    acc = acc + _dot(pending[0], wdown_ref[pending[1], :])

    x2 = x1[main] + gate2 * acc
    ms = jnp.mean(x2 * x2, axis=-1, keepdims=True)
    out_ref[0] = x2 * lax.rsqrt(ms + EPS) * fg_ref[...]


def _ffn(x, y_conv, o_f, o_b, og, mod, gn, n2, fg, w_out, w_up, dw, dwb, w_down):
    bsz, length, d = x.shape
    tm = FFN_TILE
    per = tm // HALO
    n_halo = length // HALO

    def trio(w):
        return [pl.BlockSpec((1, HALO, w), lambda b, i: (b, jnp.maximum(i * per - 1, 0), 0)),
                pl.BlockSpec((1, tm, w), lambda b, i: (b, i, 0)),
                pl.BlockSpec((1, HALO, w), lambda b, i: (b, jnp.minimum((i + 1) * per, n_halo - 1), 0))]

    def const2(shape):
        return pl.BlockSpec(shape, lambda b, i: (0, 0))

    in_specs = (trio(d) + trio(CONV_W) + trio(GLA_DV) + trio(GLA_DV) + trio(GLA_DV)
                + [pl.BlockSpec((1, 1, N_MOD * d), lambda b, i: (b, 0, 0)),
                   const2((1, GLA_DV)), const2((1, d)), const2((1, d)),
                   const2((d, d)), const2((d, 2 * FFN_HIDDEN)),
                   const2((SUBLANES, FFN_HIDDEN)), const2((1, FFN_HIDDEN)),
                   const2((FFN_HIDDEN, d))])
    return pl.pallas_call(
        _ffn_kernel,
        grid=(bsz, length // tm),
        in_specs=in_specs,
        out_specs=pl.BlockSpec((1, tm, d), lambda b, i: (b, i, 0)),
        out_shape=jax.ShapeDtypeStruct((bsz, length, d), F32),
        scratch_shapes=[pltpu.VMEM((UP_SLOTS, tm + 2 * HALO, FFN_CHUNK), F32)],
        compiler_params=pltpu.CompilerParams(dimension_semantics=("arbitrary", "arbitrary"),
                                             vmem_limit_bytes=VMEM_LIMIT),
        name="out_proj_ffn",
    )(x, x, x, y_conv, y_conv, y_conv, o_f, o_f, o_f, o_b, o_b, o_b, og, og, og,
      mod, gn, n2, fg, w_out, w_up, dw, dwb, w_down)


def kernel(x, c, ctx, c_ctx, w_mod, b_mod, norm1_g, w_in, conv_dw, conv_b, conv_ln_g, conv_ln_b,
           w_gf, b_gf, w_gb, b_gb, gla_norm_g, w_out, norm2_g, w_up, ffn_dw, ffn_dw_b, w_down, final_g):
    bsz, seq, d = x.shape
    ctx_len = ctx.shape[1]
    layer = 0

    cvec = jnp.concatenate([c, c_ctx[None, :], jnp.zeros((SUBLANES - bsz - 1, d), F32)], axis=0)
    mods = _modulation(cvec, w_mod[layer], b_mod[layer][None, :])
    mod_lat = mods[:bsz][:, None, :]
    mod_ctx = jnp.broadcast_to(mods[bsz][None, None, :], (bsz, 1, N_MOD * d))

    assert w_in.shape[2] == D_IN_PAD
    w_in_p = w_in[layer]
    zeros_g = jnp.zeros((GATE_RANK, GLA_DK), F32)
    wgf = jnp.concatenate([w_gf[layer], zeros_g], axis=0).astype(BF16)
    wgb = jnp.concatenate([zeros_g, w_gb[layer]], axis=0).astype(BF16)
    bgf = b_gf[layer][None, :]
    bgb = b_gb[layer][None, :]
    n1 = norm1_g[layer][None, :]

    k_c, v_c, z_c = _input_proj(ctx, mod_ctx, n1, w_in_p, ctx_len, full=False)
    zero_state = jnp.zeros((bsz, GLA_HEADS, GLA_HEAD_DK, GLA_HEAD_DV), F32)
    s_f, s_b = _gla(None, k_c, v_c, z_c, wgf, bgf, wgb, bgb, zero_state, zero_state)

    glu, q, k, v, og, z = _input_proj(x, mod_lat, n1, w_in_p, IN_TILE)
    o_f, o_b, _, _ = _gla(q, k, v, z, wgf, bgf, wgb, bgb, s_f, s_b)

    dw = jnp.pad(conv_dw[layer], ((0, 4 * SUBLANES - CONV_K), (0, 0)))
    y_conv = _conv_module(glu.reshape(bsz, seq // GRID_W, GRID_W, CONV_W),
                          dw[:, :CONV_HALF], dw[:, CONV_HALF:], conv_b[layer][None, :],
                          conv_ln_g[layer][None, :], conv_ln_b[layer][None, :])
    y_conv = y_conv.reshape(bsz, seq, CONV_W)

    ffn_w = jnp.pad(ffn_dw[layer], ((0, SUBLANES - ffn_dw.shape[1]), (0, 0)))
    return _ffn(x, y_conv, o_f, o_b, og, mod_lat, gla_norm_g[layer][None, :], norm2_g[layer][None, :],
                      final_g[None, :], w_out[layer].astype(BF16), w_up[layer].astype(BF16),
                      ffn_w, ffn_dw_b[layer][None, :], w_down[layer].astype(BF16))
```

```python
import functools

import jax
import jax.numpy as jnp
from jax import lax
from jax.experimental import pallas as pl
from jax.experimental.pallas import tpu as pltpu

F32 = jnp.float32
BF16 = jnp.bfloat16

D_MODEL = 1024
GRID_W = 64
CONV_W = 512
CONV_HALF = CONV_W // 2
CONV_K = 31
CONV_PAD = CONV_K // 2
GLA_HEADS = 4
GLA_DV = 512
GLA_HEAD_DV = 128
GLA_DK = 256
GLA_HEAD_DK = 64
GATE_RANK = 16
GATE_NORM = 16.0
CHUNK = 64
FFN_HIDDEN = 2816
N_MOD = 6
EPS = 1e-6

LANES = 128
SUBLANES = 8
V7X_VMEM_BYTES = 64 * 1024 * 1024
VMEM_LIMIT = V7X_VMEM_BYTES - 8 * 1024 * 1024

Z_PAD = 2 * GATE_RANK
D_IN_PAD = 2 * CONV_W + 2 * GLA_DK + 2 * GLA_DV + Z_PAD
OFF_CU, OFF_CG = 0, CONV_W
OFF_Q = 2 * CONV_W
OFF_K = OFF_Q + GLA_DK
OFF_V = OFF_K + GLA_DK
OFF_OG = OFF_V + GLA_DV
OFF_Z = OFF_OG + GLA_DV

IN_TILE = 1024
IN_SUB = 256
GLA_BLOCK = 2048
CONV_ROWS = 16
FFN_TILE = 512
FFN_CHUNK = 256
HALO = SUBLANES
UP_SLOTS = 3


def _sigmoid(x):
    return 1.0 / (1.0 + jnp.exp(-x))


def _split_bf16(x):
    hi = x.astype(BF16)
    lo = (x - hi.astype(F32)).astype(BF16)
    return hi, lo


def _dot(a, b):
    return jnp.dot(a, b, preferred_element_type=F32)


def _mod_kernel(c_ref, w_ref, b_ref, o_ref):
    c = c_ref[...]
    s = c * _sigmoid(c)
    s_hi, s_lo = _split_bf16(s)
    w = w_ref[...].astype(BF16)
    o_ref[...] = _dot(s_hi, w) + _dot(s_lo, w) + b_ref[...]


def _modulation(cvec, w_mod, b_mod):
    rows, d = cvec.shape
    n = w_mod.shape[1]
    bn = D_MODEL
    return pl.pallas_call(
        _mod_kernel,
        grid=(n // bn,),
        in_specs=[pl.BlockSpec((rows, d), lambda j: (0, 0)),
                  pl.BlockSpec((d, bn), lambda j: (0, j)),
                  pl.BlockSpec((1, bn), lambda j: (0, j))],
        out_specs=pl.BlockSpec((rows, bn), lambda j: (0, j)),
        out_shape=jax.ShapeDtypeStruct((rows, n), F32),
        compiler_params=pltpu.CompilerParams(dimension_semantics=("arbitrary",),
                                             vmem_limit_bytes=VMEM_LIMIT),
        name="modulation",
    )(cvec, w_mod, b_mod)


def _in_kernel(x_ref, mod_ref, g_ref, w_ref, *out_refs, full):
    m = mod_ref[0]
    shift = m[:, 0:D_MODEL]
    scale = m[:, D_MODEL:2 * D_MODEL]
    tm = x_ref.shape[1]
    sub = min(tm, IN_SUB)
    n_sub = tm // sub

    hs = []
    for s in range(n_sub):
        x = x_ref[0, s * sub:(s + 1) * sub, :]
        ms = jnp.mean(x * x, axis=-1, keepdims=True)
        y = x * lax.rsqrt(ms + EPS) * g_ref[...]
        hs.append((y * (1.0 + scale) + shift).astype(BF16))

    for s in range(n_sub):
        rows = slice(s * sub, (s + 1) * sub)

        def proj(lo, width):
            return _dot(hs[s], w_ref[:, lo:lo + width])

        if full:
            glu_ref, q_ref, k_ref, v_ref, og_ref, z_ref = out_refs
            glu_ref[0, rows, :] = (proj(OFF_CU, CONV_W) * _sigmoid(proj(OFF_CG, CONV_W))).astype(glu_ref.dtype)
            q_ref[0, rows, :] = proj(OFF_Q, GLA_DK).astype(q_ref.dtype)
            og_ref[0, rows, :] = proj(OFF_OG, GLA_DV).astype(og_ref.dtype)
        else:
            k_ref, v_ref, z_ref = out_refs
        k_ref[0, rows, :] = proj(OFF_K, GLA_DK).astype(k_ref.dtype)
        v_ref[0, rows, :] = proj(OFF_V, GLA_DV).astype(v_ref.dtype)
        z_ref[0, rows, :] = proj(OFF_Z, Z_PAD).astype(z_ref.dtype)


def _input_proj(x, mod, norm_g, w_in_p, tm, full=True):
    bsz, length, d = x.shape
    if full:
        widths = (CONV_W, GLA_DK, GLA_DK, GLA_DV, GLA_DV, Z_PAD)
        dtypes = (BF16, BF16, BF16, BF16, F32, BF16)
    else:
        widths = (GLA_DK, GLA_DV, Z_PAD)
        dtypes = (BF16, BF16, BF16)
    return pl.pallas_call(
        functools.partial(_in_kernel, full=full),
        grid=(bsz, length // tm),
        in_specs=[pl.BlockSpec((1, tm, d), lambda b, i: (b, i, 0)),
                  pl.BlockSpec((1, 1, N_MOD * d), lambda b, i: (b, 0, 0)),
                  pl.BlockSpec((1, d), lambda b, i: (0, 0)),
                  pl.BlockSpec((d, D_IN_PAD), lambda b, i: (0, 0))],
        out_specs=[pl.BlockSpec((1, tm, w), lambda b, i: (b, i, 0)) for w in widths],
        out_shape=[jax.ShapeDtypeStruct((bsz, length, w), dt) for w, dt in zip(widths, dtypes)],
        compiler_params=pltpu.CompilerParams(dimension_semantics=("arbitrary", "arbitrary"),
                                             vmem_limit_bytes=VMEM_LIMIT),
        name="input_proj",
    )(x, mod, norm_g, w_in_p)


def _gla_block(q_ref, k_ref, v_ref, z_ref, wg, bg, s_ref, o_ref, reverse):
    n_chunks = k_ref.shape[1] // CHUNK
    row = lax.broadcasted_iota(jnp.int32, (CHUNK, CHUNK), 0)
    col = lax.broadcasted_iota(jnp.int32, (CHUNK, CHUNK), 1)
    tri = ((col >= row) if reverse else (col <= row)).astype(BF16)
    row2 = lax.broadcasted_iota(jnp.int32, (CHUNK, LANES), 0)
    lane = lax.broadcasted_iota(jnp.int32, (CHUNK, LANES), 1)
    col2 = lane & (CHUNK - 1)
    keep_pair = (col2 >= row2) if reverse else (col2 <= row2)
    first = lane < GLA_HEAD_DK
    zero = jnp.zeros((CHUNK, LANES), BF16)

    pre = _dot(z_ref[0, :, 0:2 * GATE_RANK].astype(BF16), wg) + bg
    g = (jnp.minimum(pre, 0.0) - jnp.log1p(jnp.exp(-jnp.abs(pre)))) * (1.0 / GATE_NORM)
    g_hi, g_lo = _split_bf16(g)
    b_parts, last_parts, decays = [], [], []
    for c in range(n_chunks):
        rows = slice(c * CHUNK, (c + 1) * CHUNK)
        b_c = _dot(tri, g_hi[rows]) + _dot(tri, g_lo[rows])
        b_last = b_c[0:1] if reverse else b_c[CHUNK - 1:CHUNK]
        b_parts.append(b_c)
        last_parts.append(jnp.broadcast_to(b_last, (CHUNK, GLA_DK)))
        decays.append(jnp.broadcast_to(jnp.exp(b_last), (LANES, GLA_DK)).T)
    b = jnp.concatenate(b_parts, axis=0)
    need_out = o_ref is not None
    k = k_ref[0].astype(F32)
    if need_out:
        q_e = (q_ref[0].astype(F32) * (GLA_HEAD_DK ** -0.5) * jnp.exp(b)).astype(BF16)
        k_e = k * jnp.exp(-b)
    k_t = (k * jnp.exp(jnp.concatenate(last_parts, axis=0) - b)).astype(BF16)
    v = v_ref[0].astype(BF16)

    o_intra, kvs = [], []
    for c in range(n_chunks):
        rows = slice(c * CHUNK, (c + 1) * CHUNK)
        o_c, kv_c = [], []
        for p in range(GLA_HEADS // 2):
            lanes = slice(p * LANES, (p + 1) * LANES)
            wide = slice(2 * p * LANES, 2 * (p + 1) * LANES)
            vp = v[rows, wide]
            if need_out:
                kg = k_e[rows, lanes]
                k_blk = jnp.concatenate([jnp.where(first, kg, 0.0), jnp.where(first, 0.0, kg)],
                                        axis=0).astype(BF16)
                scores = lax.dot_general(q_e[rows, lanes], k_blk, (((1,), (1,)), ((), ())),
                                         preferred_element_type=F32)
                a = jnp.where(keep_pair, scores, 0.0).astype(BF16)
                v_blk = jnp.concatenate(
                    [jnp.concatenate([vp[:, :LANES], zero], axis=1),
                     jnp.concatenate([zero, vp[:, LANES:]], axis=1)], axis=0)
                o_c.append((a, v_blk))
            kv_c.append(lax.dot_general(k_t[rows, lanes], vp,
                                        (((0,), (0,)), ((), ())), preferred_element_type=F32))
        o_intra.append(o_c)
        kvs.append(kv_c)

    state = [s_ref[h] for h in range(GLA_HEADS)]
    zero_s = jnp.zeros((GLA_HEAD_DK, GLA_HEAD_DV), F32)
    h0 = slice(0, GLA_HEAD_DK)
    h1 = slice(GLA_HEAD_DK, 2 * GLA_HEAD_DK)
    for c in (range(n_chunks - 1, -1, -1) if reverse else range(n_chunks)):
        rows = slice(c * CHUNK, (c + 1) * CHUNK)
        for p in range(GLA_HEADS // 2):
            lanes = slice(p * LANES, (p + 1) * LANES)
            wide = slice(2 * p * LANES, 2 * (p + 1) * LANES)
            if need_out:
                s_blk = jnp.concatenate(
                    [jnp.concatenate([state[2 * p], zero_s], axis=1),
                     jnp.concatenate([zero_s, state[2 * p + 1]], axis=1)], axis=0).astype(BF16)
                a, v_blk = o_intra[c][p]
                o_ref[0, rows, wide] = _dot(jnp.concatenate([q_e[rows, lanes], a], axis=1),
                                            jnp.concatenate([s_blk, v_blk], axis=0)).astype(o_ref.dtype)
            dp = decays[c][lanes]
            kv = kvs[c][p]
            state[2 * p] = dp[h0] * state[2 * p] + kv[h0, 0:LANES]
            state[2 * p + 1] = dp[h1] * state[2 * p + 1] + kv[h1, LANES:2 * LANES]
    for h in range(GLA_HEADS):
        s_ref[h] = state[h]


def _gla_kernel(*refs, need_out):
    if need_out:
        (qf_ref, kf_ref, vf_ref, zf_ref, qb_ref, kb_ref, vb_ref, zb_ref,
         wgf_ref, bgf_ref, wgb_ref, bgb_ref, sf0_ref, sb0_ref,
         of_ref, ob_ref, sf_out_ref, sb_out_ref, sf_ref, sb_ref) = refs
    else:
        (kf_ref, vf_ref, zf_ref, kb_ref, vb_ref, zb_ref,
         wgf_ref, bgf_ref, wgb_ref, bgb_ref, sf0_ref, sb0_ref,
         sf_out_ref, sb_out_ref, sf_ref, sb_ref) = refs
        qf_ref = qb_ref = of_ref = ob_ref = None
    i = pl.program_id(1)

    @pl.when(i == 0)
    def _():
        sf_ref[...] = sf0_ref[0]
        sb_ref[...] = sb0_ref[0]

    _gla_block(qf_ref, kf_ref, vf_ref, zf_ref, wgf_ref[...], bgf_ref[...], sf_ref, of_ref, False)
    _gla_block(qb_ref, kb_ref, vb_ref, zb_ref, wgb_ref[...], bgb_ref[...], sb_ref, ob_ref, True)

    @pl.when(i == pl.num_programs(1) - 1)
    def _():
        sf_out_ref[0] = sf_ref[...]
        sb_out_ref[0] = sb_ref[...]


def _gla(q, k, v, z, wgf, bgf, wgb, bgb, sf0, sb0):
    need_out = q is not None
    bsz, length, _ = k.shape
    tb = min(GLA_BLOCK, length)
    assert length % tb == 0 and tb % CHUNK == 0
    nb = length // tb

    def fwd(w):
        return pl.BlockSpec((1, tb, w), lambda b, i: (b, i, 0))

    def bwd(w):
        return pl.BlockSpec((1, tb, w), lambda b, i: (b, nb - 1 - i, 0))

    def const2(shape):
        return pl.BlockSpec(shape, lambda b, i: (0, 0))

    state = pl.BlockSpec((1, GLA_HEADS, GLA_HEAD_DK, GLA_HEAD_DV), lambda b, i: (b, 0, 0, 0))
    state_shape = jax.ShapeDtypeStruct((bsz, GLA_HEADS, GLA_HEAD_DK, GLA_HEAD_DV), F32)
    o_shape = jax.ShapeDtypeStruct((bsz, length, GLA_DV), F32)
    q_f = [fwd(GLA_DK)] if need_out else []
    q_b = [bwd(GLA_DK)] if need_out else []
    q_arg = [q] if need_out else []
    return pl.pallas_call(
        functools.partial(_gla_kernel, need_out=need_out),
        grid=(bsz, nb),
        in_specs=(q_f + [fwd(GLA_DK), fwd(GLA_DV), fwd(Z_PAD)]
                  + q_b + [bwd(GLA_DK), bwd(GLA_DV), bwd(Z_PAD)]
                  + [const2((2 * GATE_RANK, GLA_DK)), const2((1, GLA_DK)),
                     const2((2 * GATE_RANK, GLA_DK)), const2((1, GLA_DK)),
                     state, state]),
        out_specs=([fwd(GLA_DV), bwd(GLA_DV)] if need_out else []) + [state, state],
        out_shape=([o_shape, o_shape] if need_out else []) + [state_shape, state_shape],
        scratch_shapes=[pltpu.VMEM((GLA_HEADS, GLA_HEAD_DK, GLA_HEAD_DV), F32),
                        pltpu.VMEM((GLA_HEADS, GLA_HEAD_DK, GLA_HEAD_DV), F32)],
        compiler_params=pltpu.CompilerParams(dimension_semantics=("arbitrary", "arbitrary"),
                                             vmem_limit_bytes=VMEM_LIMIT),
        name="gla",
    )(*q_arg, k, v, z, *q_arg, k, v, z, wgf, bgf, wgb, bgb, sf0, sb0)


def _conv_kernel(grow_ref, gcol_ref, wrow_ref, wcol_ref, cb_ref, lg_ref, lb_ref, y_ref,
                 rowpad_ref, colpad_ref, yrow_ref):
    t = pl.program_id(1)
    n_rows = gcol_ref.shape[1]
    pad_rows = CONV_ROWS
    pad_w = 2 * SUBLANES

    @pl.when(t == 0)
    def _():
        rowpad_ref[...] = jnp.zeros(rowpad_ref.shape, F32)
        zeros = jnp.zeros((pad_rows, GRID_W, CONV_HALF), F32)
        colpad_ref[0:pad_rows] = zeros
        colpad_ref[pad_rows + n_rows:pad_rows + n_rows + pad_rows] = zeros

        def copy(j, carry):
            colpad_ref[pl.ds(pad_rows + j * CONV_ROWS, CONV_ROWS)] = (
                gcol_ref[0, pl.ds(j * CONV_ROWS, CONV_ROWS)].astype(F32))
            return carry
        lax.fori_loop(0, n_rows // CONV_ROWS, copy, 0)

    for lg in range(CONV_HALF // LANES):
        rowpad_ref[lg, :, pad_w:pad_w + GRID_W, :] = grow_ref[0, :, :, lg * LANES:(lg + 1) * LANES].astype(F32)

    half_w = GRID_W // 2

    def row_body(r, carry):
        pieces = []
        for lg in range(CONV_HALF // LANES):
            lanes = slice(lg * LANES, (lg + 1) * LANES)
            for parity in range(2):
                acc = jnp.broadcast_to(cb_ref[:, lanes], (half_w, LANES))
                for kk in range(CONV_K):
                    start = pad_w + kk - CONV_PAD + parity
                    acc = acc + (wrow_ref[kk:kk + 1, lanes]
                                 * rowpad_ref[lg, r, pl.ds(start, half_w, stride=2), :])
                yrow_ref[lg, pl.ds(parity, half_w, stride=2), :] = acc
            pieces.append(yrow_ref[lg])
        for lg in range(CONV_HALF // LANES):
            lanes = slice(lg * LANES, (lg + 1) * LANES)
            acc = jnp.broadcast_to(cb_ref[:, CONV_HALF + lg * LANES:CONV_HALF + (lg + 1) * LANES],
                                   (GRID_W, LANES))
            for kk in range(CONV_K):
                src = t * CONV_ROWS + r + (pad_rows + kk - CONV_PAD)
                acc = acc + wcol_ref[kk:kk + 1, lanes] * colpad_ref[src, :, lanes]
            pieces.append(acc)
        y = jnp.concatenate(pieces, axis=1)
        mu = jnp.mean(y, axis=-1, keepdims=True)
        var = jnp.mean(y * y, axis=-1, keepdims=True) - mu * mu
        yn = (y - mu) * lax.rsqrt(var + EPS) * lg_ref[...] + lb_ref[...]
        y_ref[0, r] = (yn * _sigmoid(yn)).astype(y_ref.dtype)
        return carry

    lax.fori_loop(0, CONV_ROWS, row_body, 0, unroll=2)


def _conv_module(glu4, w_row, w_col, conv_b, ln_g, ln_b):
    bsz, n_rows, gw, _ = glu4.shape
    return pl.pallas_call(
        _conv_kernel,
        grid=(bsz, n_rows // CONV_ROWS),
        in_specs=[pl.BlockSpec((1, CONV_ROWS, gw, CONV_HALF), lambda b, t: (b, t, 0, 0)),
                  pl.BlockSpec((1, n_rows, gw, CONV_HALF), lambda b, t: (b, 0, 0, 1)),
                  pl.BlockSpec((4 * SUBLANES, CONV_HALF), lambda b, t: (0, 0)),
                  pl.BlockSpec((4 * SUBLANES, CONV_HALF), lambda b, t: (0, 0)),
                  pl.BlockSpec((1, CONV_W), lambda b, t: (0, 0)),
                  pl.BlockSpec((1, CONV_W), lambda b, t: (0, 0)),
                  pl.BlockSpec((1, CONV_W), lambda b, t: (0, 0))],
        out_specs=pl.BlockSpec((1, CONV_ROWS, gw, CONV_W), lambda b, t: (b, t, 0, 0)),
        out_shape=jax.ShapeDtypeStruct((bsz, n_rows, gw, CONV_W), F32),
        scratch_shapes=[pltpu.VMEM((CONV_HALF // LANES, CONV_ROWS, gw + 4 * SUBLANES, LANES), F32),
                        pltpu.VMEM((n_rows + 2 * CONV_ROWS, gw, CONV_HALF), F32),
                        pltpu.VMEM((CONV_HALF // LANES, gw, LANES), F32)],
        compiler_params=pltpu.CompilerParams(dimension_semantics=("arbitrary", "arbitrary"),
                                             vmem_limit_bytes=VMEM_LIMIT),
        name="conv_module",
    )(glu4, glu4, w_row, w_col, conv_b, ln_g, ln_b)


def _ffn_kernel(x_p, x_m, x_n, yc_p, yc_m, yc_n, of_p, of_m, of_n, ob_p, ob_m, ob_n,
                      og_p, og_m, og_n, mod_ref, gn_ref, n2_ref, fg_ref, wout_ref, wup_ref,
                      dw_ref, dwb_ref, wdown_ref, out_ref, a_ref):
    i = pl.program_id(1)
    last = pl.num_programs(1) - 1
    tm = x_m.shape[1]
    main = slice(HALO, HALO + tm)

    def ext(p, m, n):
        return jnp.concatenate([p[0], m[0], n[0]], axis=0)

    mod = mod_ref[0]
    gate1 = mod[:, 2 * D_MODEL:3 * D_MODEL]
    shift2 = mod[:, 3 * D_MODEL:4 * D_MODEL]
    scale2 = mod[:, 4 * D_MODEL:5 * D_MODEL]
    gate2 = mod[:, 5 * D_MODEL:6 * D_MODEL]

    o = ext(of_p, of_m, of_n) + ext(ob_p, ob_m, ob_n)
    og = ext(og_p, og_m, og_n)
    parts = [ext(yc_p, yc_m, yc_n).astype(BF16)]
    for h in range(GLA_HEADS):
        lanes = slice(h * GLA_HEAD_DV, (h + 1) * GLA_HEAD_DV)
        oh = o[:, lanes]
        ms = jnp.mean(oh * oh, axis=-1, keepdims=True)
        ogh = og[:, lanes]
        parts.append((oh * lax.rsqrt(ms + EPS) * gn_ref[:, lanes] * (ogh * _sigmoid(ogh))).astype(BF16))
    mix = jnp.concatenate(parts, axis=1)
    x1 = ext(x_p, x_m, x_n) + gate1 * _dot(mix, wout_ref[...])

    ms = jnp.mean(x1 * x1, axis=-1, keepdims=True)
    h2_f32 = (x1 * lax.rsqrt(ms + EPS) * n2_ref[...]) * (1.0 + scale2) + shift2
    h2 = h2_f32.astype(BF16)
    h2_main = h2_f32[main].astype(BF16)

    row = lax.broadcasted_iota(jnp.int32, (tm, 1), 0)
    kill_prev = jnp.logical_and(row == 0, i == 0)
    kill_next = jnp.logical_and(row == tm - 1, i == last)
    n_chunks = FFN_HIDDEN // FFN_CHUNK

    def up(j):
        a_ref[j % UP_SLOTS] = _dot(h2, wup_ref[:, j * FFN_CHUNK:(j + 1) * FFN_CHUNK])
        return _dot(h2_main, wup_ref[:, FFN_HIDDEN + j * FFN_CHUNK:FFN_HIDDEN + (j + 1) * FFN_CHUNK])

    acc = None
    vals = {j: up(j) for j in range(UP_SLOTS - 1)}
    pending = None
    for j in range(n_chunks):
        if j + UP_SLOTS - 1 < n_chunks:
            vals[j + UP_SLOTS - 1] = up(j + UP_SLOTS - 1)
        if pending is not None:
            part = _dot(pending[0], wdown_ref[pending[1], :])
            acc = part if acc is None else acc + part
        val = vals.pop(j)
        cols = slice(j * FFN_CHUNK, (j + 1) * FFN_CHUNK)
        a_cur = a_ref.at[j % UP_SLOTS]
        a_prev = jnp.where(kill_prev, 0.0, a_cur[HALO - 1:HALO - 1 + tm, :])
        a_next = jnp.where(kill_next, 0.0, a_cur[HALO + 1:HALO + 1 + tm, :])
        ac = (dw_ref[0:1, cols] * a_prev + dw_ref[1:2, cols] * a_cur[main, :]
              + dw_ref[2:3, cols] * a_next + dwb_ref[:, cols])
        f = (ac * _sigmoid(ac) * val).astype(BF16)
        pending = (f, cols)
    acc = acc + _dot(pending[0], wdown_ref[pending[1], :])

    x2 = x1[main] + gate2 * acc
    ms = jnp.mean(x2 * x2, axis=-1, keepdims=True)
    out_ref[0] = x2 * lax.rsqrt(ms + EPS) * fg_ref[...]


def _ffn(x, y_conv, o_f, o_b, og, mod, gn, n2, fg, w_out, w_up, dw, dwb, w_down):
    bsz, length, d = x.shape
    tm = FFN_TILE
    per = tm // HALO
    n_halo = length // HALO

    def trio(w):
        return [pl.BlockSpec((1, HALO, w), lambda b, i: (b, jnp.maximum(i * per - 1, 0), 0)),
                pl.BlockSpec((1, tm, w), lambda b, i: (b, i, 0)),
                pl.BlockSpec((1, HALO, w), lambda b, i: (b, jnp.minimum((i + 1) * per, n_halo - 1), 0))]

    def const2(shape):
        return pl.BlockSpec(shape, lambda b, i: (0, 0))

    in_specs = (trio(d) + trio(CONV_W) + trio(GLA_DV) + trio(GLA_DV) + trio(GLA_DV)
                + [pl.BlockSpec((1, 1, N_MOD * d), lambda b, i: (b, 0, 0)),
                   const2((1, GLA_DV)), const2((1, d)), const2((1, d)),
                   const2((d, d)), const2((d, 2 * FFN_HIDDEN)),
                   const2((SUBLANES, FFN_HIDDEN)), const2((1, FFN_HIDDEN)),
                   const2((FFN_HIDDEN, d))])
    return pl.pallas_call(
        _ffn_kernel,
        grid=(bsz, length // tm),
        in_specs=in_specs,
        out_specs=pl.BlockSpec((1, tm, d), lambda b, i: (b, i, 0)),
        out_shape=jax.ShapeDtypeStruct((bsz, length, d), F32),
        scratch_shapes=[pltpu.VMEM((UP_SLOTS, tm + 2 * HALO, FFN_CHUNK), F32)],
        compiler_params=pltpu.CompilerParams(dimension_semantics=("arbitrary", "arbitrary"),
                                             vmem_limit_bytes=VMEM_LIMIT),
        name="out_proj_ffn",
    )(x, x, x, y_conv, y_conv, y_conv, o_f, o_f, o_f, o_b, o_b, o_b, og, og, og,
      mod, gn, n2, fg, w_out, w_up, dw, dwb, w_down)


def kernel(x, c, ctx, c_ctx, w_mod, b_mod, norm1_g, w_in, conv_dw, conv_b, conv_ln_g, conv_ln_b,
           w_gf, b_gf, w_gb, b_gb, gla_norm_g, w_out, norm2_g, w_up, ffn_dw, ffn_dw_b, w_down, final_g):
    bsz, seq, d = x.shape
    ctx_len = ctx.shape[1]
    layer = 0

    cvec = jnp.concatenate([c, c_ctx[None, :], jnp.zeros((SUBLANES - bsz - 1, d), F32)], axis=0)
    mods = _modulation(cvec, w_mod[layer], b_mod[layer][None, :])
    mod_lat = mods[:bsz][:, None, :]
    mod_ctx = jnp.broadcast_to(mods[bsz][None, None, :], (bsz, 1, N_MOD * d))

    assert w_in.shape[2] == D_IN_PAD
    w_in_p = w_in[layer]
    zeros_g = jnp.zeros((GATE_RANK, GLA_DK), F32)
    wgf = jnp.concatenate([w_gf[layer], zeros_g], axis=0).astype(BF16)
    wgb = jnp.concatenate([zeros_g, w_gb[layer]], axis=0).astype(BF16)
    bgf = b_gf[layer][None, :]
    bgb = b_gb[layer][None, :]
    n1 = norm1_g[layer][None, :]

    k_c, v_c, z_c = _input_proj(ctx, mod_ctx, n1, w_in_p, ctx_len, full=False)
    zero_state = jnp.zeros((bsz, GLA_HEADS, GLA_HEAD_DK, GLA_HEAD_DV), F32)
    s_f, s_b = _gla(None, k_c, v_c, z_c, wgf, bgf, wgb, bgb, zero_state, zero_state)

    glu, q, k, v, og, z = _input_proj(x, mod_lat, n1, w_in_p, IN_TILE)
    o_f, o_b, _, _ = _gla(q, k, v, z, wgf, bgf, wgb, bgb, s_f, s_b)

    dw = jnp.pad(conv_dw[layer], ((0, 4 * SUBLANES - CONV_K), (0, 0)))
    y_conv = _conv_module(glu.reshape(bsz, seq // GRID_W, GRID_W, CONV_W),
                          dw[:, :CONV_HALF], dw[:, CONV_HALF:], conv_b[layer][None, :],
                          conv_ln_g[layer][None, :], conv_ln_b[layer][None, :])
    y_conv = y_conv.reshape(bsz, seq, CONV_W)

    ffn_w = jnp.pad(ffn_dw[layer], ((0, SUBLANES - ffn_dw.shape[1]), (0, 0)))
    return _ffn(x, y_conv, o_f, o_b, og, mod_lat, gla_norm_g[layer][None, :], norm2_g[layer][None, :],
                      final_g[None, :], w_out[layer].astype(BF16), w_up[layer].astype(BF16),
                      ffn_w, ffn_dw_b[layer][None, :], w_down[layer].astype(BF16))
```

```python
import functools

import jax
import jax.numpy as jnp
from jax import lax
from jax.experimental import pallas as pl
from jax.experimental.pallas import tpu as pltpu

F32 = jnp.float32
BF16 = jnp.bfloat16

D_MODEL = 1024
GRID_W = 64
CONV_W = 512
CONV_HALF = CONV_W // 2
CONV_K = 31
CONV_PAD = CONV_K // 2
GLA_HEADS = 4
GLA_DV = 512
GLA_HEAD_DV = 128
GLA_DK = 256
GLA_HEAD_DK = 64
GATE_RANK = 16
GATE_NORM = 16.0
CHUNK = 64
FFN_HIDDEN = 2816
N_MOD = 6
EPS = 1e-6

LANES = 128
SUBLANES = 8
BF16_ROWS = 2 * SUBLANES
V7X_VMEM_BYTES = 64 * 1024 * 1024
VMEM_LIMIT = V7X_VMEM_BYTES - 8 * 1024 * 1024

Z_PAD = 2 * GATE_RANK
D_IN_PAD = 2 * CONV_W + 2 * GLA_DK + 2 * GLA_DV + Z_PAD
OFF_CU, OFF_CG = 0, CONV_W
OFF_Q = 2 * CONV_W
OFF_K = OFF_Q + GLA_DK
OFF_V = OFF_K + GLA_DK
OFF_OG = OFF_V + GLA_DV
OFF_Z = OFF_OG + GLA_DV

IN_TILE = 1024
IN_SUB = 256
GLA_BLOCK = 2048
CONV_ROWS = 16
FFN_TILE = 512
FFN_CHUNK = 256
HALO = SUBLANES
UP_SLOTS = 3


def _sigmoid(x):
    return 1.0 / (1.0 + jnp.exp(-x))


def _split_bf16(x):
    hi = x.astype(BF16)
    lo = (x - hi.astype(F32)).astype(BF16)
    return hi, lo


def _dot(a, b):
    return jnp.dot(a, b, preferred_element_type=F32)


def _mod_kernel(c_ref, w_ref, b_ref, o_ref):
    c = c_ref[...]
    s = c * _sigmoid(c)
    s_hi, s_lo = _split_bf16(s)
    w = w_ref[...].astype(BF16)
    o_ref[...] = _dot(s_hi, w) + _dot(s_lo, w) + b_ref[...]


def _modulation(cvec, w_mod, b_mod):
    rows, d = cvec.shape
    n = w_mod.shape[1]
    bn = D_MODEL
    return pl.pallas_call(
        _mod_kernel,
        grid=(n // bn,),
        in_specs=[pl.BlockSpec((rows, d), lambda j: (0, 0)),
                  pl.BlockSpec((d, bn), lambda j: (0, j)),
                  pl.BlockSpec((1, bn), lambda j: (0, j))],
        out_specs=pl.BlockSpec((rows, bn), lambda j: (0, j)),
        out_shape=jax.ShapeDtypeStruct((rows, n), F32),
        compiler_params=pltpu.CompilerParams(dimension_semantics=("arbitrary",),
                                             vmem_limit_bytes=VMEM_LIMIT),
        name="modulation",
    )(cvec, w_mod, b_mod)


def _in_kernel(x_ref, mod_ref, g_ref, w_ref, *out_refs, full):
    m = mod_ref[0]
    shift = m[:, 0:D_MODEL]
    scale = m[:, D_MODEL:2 * D_MODEL]
    tm = x_ref.shape[1]
    sub = min(tm, IN_SUB)
    n_sub = tm // sub

    hs = []
    for s in range(n_sub):
        x = x_ref[0, s * sub:(s + 1) * sub, :]
        ms = jnp.mean(x * x, axis=-1, keepdims=True)
        y = x * lax.rsqrt(ms + EPS) * g_ref[...]
        hs.append((y * (1.0 + scale) + shift).astype(BF16))

    groups = ([(OFF_CU, CONV_W), (OFF_CG, CONV_W), (OFF_Q, GLA_DK), (OFF_OG, GLA_DV)] if full else []) + [
        (OFF_K, GLA_DK), (OFF_V, GLA_DV), (OFF_Z, Z_PAD)]
    w_bf = {lo: w_ref[:, lo:lo + width].astype(BF16) for lo, width in groups}

    for s in range(n_sub):
        rows = slice(s * sub, (s + 1) * sub)

        def proj(lo, width):
            return _dot(hs[s], w_bf[lo])

        if full:
            glu_ref, q_ref, k_ref, v_ref, og_ref, z_ref = out_refs
            cu, cg, q_val = proj(OFF_CU, CONV_W), proj(OFF_CG, CONV_W), proj(OFF_Q, GLA_DK)
            glu_ref[0, rows, :] = (cu * _sigmoid(cg)).astype(glu_ref.dtype)
            q_ref[0, rows, :] = q_val.astype(q_ref.dtype)
            og_ref[0, rows, :] = proj(OFF_OG, GLA_DV).astype(og_ref.dtype)
        else:
            k_ref, v_ref, z_ref = out_refs
        k_ref[0, rows, :] = proj(OFF_K, GLA_DK).astype(k_ref.dtype)
        v_ref[0, rows, :] = proj(OFF_V, GLA_DV).astype(v_ref.dtype)
        z_ref[0, rows, :] = proj(OFF_Z, Z_PAD).astype(z_ref.dtype)


def _input_proj(x, mod, norm_g, w_in_p, tm, full=True):
    bsz, length, d = x.shape
    if full:
        widths = (CONV_W, GLA_DK, GLA_DK, GLA_DV, GLA_DV, Z_PAD)
        dtypes = (BF16, BF16, BF16, BF16, F32, BF16)
    else:
        widths = (GLA_DK, GLA_DV, Z_PAD)
        dtypes = (BF16, BF16, BF16)
    return pl.pallas_call(
        functools.partial(_in_kernel, full=full),
        grid=(bsz, length // tm),
        in_specs=[pl.BlockSpec((1, tm, d), lambda b, i: (b, i, 0)),
                  pl.BlockSpec((1, 1, N_MOD * d), lambda b, i: (b, 0, 0)),
                  pl.BlockSpec((1, d), lambda b, i: (0, 0)),
                  pl.BlockSpec((d, D_IN_PAD), lambda b, i: (0, 0))],
        out_specs=[pl.BlockSpec((1, tm, w), lambda b, i: (b, i, 0)) for w in widths],
        out_shape=[jax.ShapeDtypeStruct((bsz, length, w), dt) for w, dt in zip(widths, dtypes)],
        compiler_params=pltpu.CompilerParams(dimension_semantics=("arbitrary", "arbitrary"),
                                             vmem_limit_bytes=VMEM_LIMIT),
        name="input_proj",
    )(x, mod, norm_g, w_in_p)


def _gla_block(q_ref, k_ref, v_ref, z_ref, wg, bg, s_ref, o_ref, reverse):
    n_chunks = k_ref.shape[1] // CHUNK
    row = lax.broadcasted_iota(jnp.int32, (CHUNK, CHUNK), 0)
    col = lax.broadcasted_iota(jnp.int32, (CHUNK, CHUNK), 1)
    tri = ((col >= row) if reverse else (col <= row)).astype(BF16)
    row2 = lax.broadcasted_iota(jnp.int32, (CHUNK, LANES), 0)
    lane = lax.broadcasted_iota(jnp.int32, (CHUNK, LANES), 1)
    col2 = lane & (CHUNK - 1)
    keep_pair = (col2 >= row2) if reverse else (col2 <= row2)
    first = lane < GLA_HEAD_DK
    zero = jnp.zeros((CHUNK, LANES), BF16)

    pre = _dot(z_ref[0, :, 0:2 * GATE_RANK].astype(BF16), wg) + bg
    g = (jnp.minimum(pre, 0.0) - jnp.log1p(jnp.exp(-jnp.abs(pre)))) * (1.0 / GATE_NORM)
    g_hi, g_lo = _split_bf16(g)
    b_parts, last_parts, decays = [], [], []
    for c in range(n_chunks):
        rows = slice(c * CHUNK, (c + 1) * CHUNK)
        b_c = _dot(tri, g_hi[rows]) + _dot(tri, g_lo[rows])
        b_last = b_c[0:1] if reverse else b_c[CHUNK - 1:CHUNK]
        b_parts.append(b_c)
        last_parts.append(jnp.broadcast_to(b_last, (CHUNK, GLA_DK)))
        decays.append(jnp.broadcast_to(jnp.exp(b_last), (LANES, GLA_DK)).T)
    b = jnp.concatenate(b_parts, axis=0)
    need_out = o_ref is not None
    k = k_ref[0].astype(F32)
    if need_out:
        q_e = (q_ref[0].astype(F32) * (GLA_HEAD_DK ** -0.5) * jnp.exp(b)).astype(BF16)
        k_e = k * jnp.exp(-b)
    k_t = (k * jnp.exp(jnp.concatenate(last_parts, axis=0) - b)).astype(BF16)
    v = v_ref[0].astype(BF16)

    o_intra, kvs = [], []
    for c in range(n_chunks):
        rows = slice(c * CHUNK, (c + 1) * CHUNK)
        o_c, kv_c = [], []
        for p in range(GLA_HEADS // 2):
            lanes = slice(p * LANES, (p + 1) * LANES)
            wide = slice(2 * p * LANES, 2 * (p + 1) * LANES)
            vp = v[rows, wide]
            if need_out:
                kg = k_e[rows, lanes]
                k_blk = jnp.concatenate([jnp.where(first, kg, 0.0), jnp.where(first, 0.0, kg)],
                                        axis=0).astype(BF16)
                scores = lax.dot_general(q_e[rows, lanes], k_blk, (((1,), (1,)), ((), ())),
                                         preferred_element_type=F32)
                a = jnp.where(keep_pair, scores, 0.0).astype(BF16)
                v_blk = jnp.concatenate(
                    [jnp.concatenate([vp[:, :LANES], zero], axis=1),
                     jnp.concatenate([zero, vp[:, LANES:]], axis=1)], axis=0)
                o_c.append((a, v_blk))
            kv_c.append(lax.dot_general(k_t[rows, lanes], vp,
                                        (((0,), (0,)), ((), ())), preferred_element_type=F32))
        o_intra.append(o_c)
        kvs.append(kv_c)

    state = [s_ref[h] for h in range(GLA_HEADS)]
    zero_s = jnp.zeros((GLA_HEAD_DK, GLA_HEAD_DV), F32)
    h0 = slice(0, GLA_HEAD_DK)
    h1 = slice(GLA_HEAD_DK, 2 * GLA_HEAD_DK)
    for c in (range(n_chunks - 1, -1, -1) if reverse else range(n_chunks)):
        rows = slice(c * CHUNK, (c + 1) * CHUNK)
        for p in range(GLA_HEADS // 2):
            lanes = slice(p * LANES, (p + 1) * LANES)
            wide = slice(2 * p * LANES, 2 * (p + 1) * LANES)
            if need_out:
                s_blk = jnp.concatenate(
                    [jnp.concatenate([state[2 * p], zero_s], axis=1),
                     jnp.concatenate([zero_s, state[2 * p + 1]], axis=1)], axis=0).astype(BF16)
                a, v_blk = o_intra[c][p]
                o_ref[0, rows, wide] = _dot(jnp.concatenate([q_e[rows, lanes], a], axis=1),
                                            jnp.concatenate([s_blk, v_blk], axis=0)).astype(o_ref.dtype)
            dp = decays[c][lanes]
            kv = kvs[c][p]
            state[2 * p] = dp[h0] * state[2 * p] + kv[h0, 0:LANES]
            state[2 * p + 1] = dp[h1] * state[2 * p + 1] + kv[h1, LANES:2 * LANES]
    for h in range(GLA_HEADS):
        s_ref[h] = state[h]


def _gla_kernel(*refs, need_out):
    if need_out:
        (qf_ref, kf_ref, vf_ref, zf_ref, qb_ref, kb_ref, vb_ref, zb_ref,
         wgf_ref, bgf_ref, wgb_ref, bgb_ref, sf0_ref, sb0_ref,
         of_ref, ob_ref, sf_out_ref, sb_out_ref, sf_ref, sb_ref) = refs
    else:
        (kf_ref, vf_ref, zf_ref, kb_ref, vb_ref, zb_ref,
         wgf_ref, bgf_ref, wgb_ref, bgb_ref, sf0_ref, sb0_ref,
         sf_out_ref, sb_out_ref, sf_ref, sb_ref) = refs
        qf_ref = qb_ref = of_ref = ob_ref = None
    i = pl.program_id(1)

    @pl.when(i == 0)
    def _():
        sf_ref[...] = sf0_ref[0]
        sb_ref[...] = sb0_ref[0]

    _gla_block(qf_ref, kf_ref, vf_ref, zf_ref, wgf_ref[...], bgf_ref[...], sf_ref, of_ref, False)
    _gla_block(qb_ref, kb_ref, vb_ref, zb_ref, wgb_ref[...], bgb_ref[...], sb_ref, ob_ref, True)

    @pl.when(i == pl.num_programs(1) - 1)
    def _():
        sf_out_ref[0] = sf_ref[...]
        sb_out_ref[0] = sb_ref[...]


def _gla(q, k, v, z, wgf, bgf, wgb, bgb, sf0, sb0):
    need_out = q is not None
    bsz, length, _ = k.shape
    tb = min(GLA_BLOCK, length)
    assert length % tb == 0 and tb % CHUNK == 0
    nb = length // tb

    def fwd(w):
        return pl.BlockSpec((1, tb, w), lambda b, i: (b, i, 0))

    def bwd(w):
        return pl.BlockSpec((1, tb, w), lambda b, i: (b, nb - 1 - i, 0))

    def const2(shape):
        return pl.BlockSpec(shape, lambda b, i: (0, 0))

    state = pl.BlockSpec((1, GLA_HEADS, GLA_HEAD_DK, GLA_HEAD_DV), lambda b, i: (b, 0, 0, 0))
    state_shape = jax.ShapeDtypeStruct((bsz, GLA_HEADS, GLA_HEAD_DK, GLA_HEAD_DV), F32)
    o_shape = jax.ShapeDtypeStruct((bsz, length, GLA_DV), BF16)
    q_f = [fwd(GLA_DK)] if need_out else []
    q_b = [bwd(GLA_DK)] if need_out else []
    q_arg = [q] if need_out else []
    return pl.pallas_call(
        functools.partial(_gla_kernel, need_out=need_out),
        grid=(bsz, nb),
        in_specs=(q_f + [fwd(GLA_DK), fwd(GLA_DV), fwd(Z_PAD)]
                  + q_b + [bwd(GLA_DK), bwd(GLA_DV), bwd(Z_PAD)]
                  + [const2((2 * GATE_RANK, GLA_DK)), const2((1, GLA_DK)),
                     const2((2 * GATE_RANK, GLA_DK)), const2((1, GLA_DK)),
                     state, state]),
        out_specs=([fwd(GLA_DV), bwd(GLA_DV)] if need_out else []) + [state, state],
        out_shape=([o_shape, o_shape] if need_out else []) + [state_shape, state_shape],
        scratch_shapes=[pltpu.VMEM((GLA_HEADS, GLA_HEAD_DK, GLA_HEAD_DV), F32),
                        pltpu.VMEM((GLA_HEADS, GLA_HEAD_DK, GLA_HEAD_DV), F32)],
        compiler_params=pltpu.CompilerParams(dimension_semantics=("arbitrary", "arbitrary"),
                                             vmem_limit_bytes=VMEM_LIMIT),
        name="gla",
    )(*q_arg, k, v, z, *q_arg, k, v, z, wgf, bgf, wgb, bgb, sf0, sb0)


def _conv_kernel(grow_ref, gcol_ref, wrow_ref, wcol_ref, cb_ref, lg_ref, lb_ref, y_ref,
                 rowpad_ref, colpad_ref, yrow_ref):
    t = pl.program_id(1)
    n_rows = gcol_ref.shape[1]
    pad_rows = CONV_ROWS
    pad_w = 2 * SUBLANES

    @pl.when(t == 0)
    def _():
        rowpad_ref[...] = jnp.zeros(rowpad_ref.shape, F32)
        zeros = jnp.zeros((pad_rows, GRID_W, CONV_HALF), F32)
        colpad_ref[0:pad_rows] = zeros
        colpad_ref[pad_rows + n_rows:pad_rows + n_rows + pad_rows] = zeros

        def copy(j, carry):
            colpad_ref[pl.ds(pad_rows + j * CONV_ROWS, CONV_ROWS)] = (
                gcol_ref[0, pl.ds(j * CONV_ROWS, CONV_ROWS)].astype(F32))
            return carry
        lax.fori_loop(0, n_rows // CONV_ROWS, copy, 0)

    for lg in range(CONV_HALF // LANES):
        rowpad_ref[lg, :, pad_w:pad_w + GRID_W, :] = grow_ref[0, :, :, lg * LANES:(lg + 1) * LANES].astype(F32)

    half_w = GRID_W // 2

    def row_body(r, carry):
        pieces = []
        for lg in range(CONV_HALF // LANES):
            lanes = slice(lg * LANES, (lg + 1) * LANES)
            for parity in range(2):
                acc = jnp.broadcast_to(cb_ref[:, lanes], (half_w, LANES))
                for kk in range(CONV_K):
                    start = pad_w + kk - CONV_PAD + parity
                    acc = acc + (wrow_ref[kk:kk + 1, lanes]
                                 * rowpad_ref[lg, r, pl.ds(start, half_w, stride=2), :])
                yrow_ref[lg, pl.ds(parity, half_w, stride=2), :] = acc
            pieces.append(yrow_ref[lg])
        for lg in range(CONV_HALF // LANES):
            lanes = slice(lg * LANES, (lg + 1) * LANES)
            acc = jnp.broadcast_to(cb_ref[:, CONV_HALF + lg * LANES:CONV_HALF + (lg + 1) * LANES],
                                   (GRID_W, LANES))
            for kk in range(CONV_K):
                src = t * CONV_ROWS + r + (pad_rows + kk - CONV_PAD)
                acc = acc + wcol_ref[kk:kk + 1, lanes] * colpad_ref[src, :, lanes]
            pieces.append(acc)
        y = jnp.concatenate(pieces, axis=1)
        mu = jnp.mean(y, axis=-1, keepdims=True)
        var = jnp.mean(y * y, axis=-1, keepdims=True) - mu * mu
        yn = (y - mu) * lax.rsqrt(var + EPS) * lg_ref[...] + lb_ref[...]
        y_ref[0, r] = (yn * _sigmoid(yn)).astype(y_ref.dtype)
        return carry

    lax.fori_loop(0, CONV_ROWS, row_body, 0, unroll=2)


def _conv_module(glu4, w_row, w_col, conv_b, ln_g, ln_b):
    bsz, n_rows, gw, _ = glu4.shape
    return pl.pallas_call(
        _conv_kernel,
        grid=(bsz, n_rows // CONV_ROWS),
        in_specs=[pl.BlockSpec((1, CONV_ROWS, gw, CONV_HALF), lambda b, t: (b, t, 0, 0)),
                  pl.BlockSpec((1, n_rows, gw, CONV_HALF), lambda b, t: (b, 0, 0, 1)),
                  pl.BlockSpec((4 * SUBLANES, CONV_HALF), lambda b, t: (0, 0)),
                  pl.BlockSpec((4 * SUBLANES, CONV_HALF), lambda b, t: (0, 0)),
                  pl.BlockSpec((1, CONV_W), lambda b, t: (0, 0)),
                  pl.BlockSpec((1, CONV_W), lambda b, t: (0, 0)),
                  pl.BlockSpec((1, CONV_W), lambda b, t: (0, 0))],
        out_specs=pl.BlockSpec((1, CONV_ROWS, gw, CONV_W), lambda b, t: (b, t, 0, 0)),
        out_shape=jax.ShapeDtypeStruct((bsz, n_rows, gw, CONV_W), F32),
        scratch_shapes=[pltpu.VMEM((CONV_HALF // LANES, CONV_ROWS, gw + 4 * SUBLANES, LANES), F32),
                        pltpu.VMEM((n_rows + 2 * CONV_ROWS, gw, CONV_HALF), F32),
                        pltpu.VMEM((CONV_HALF // LANES, gw, LANES), F32)],
        compiler_params=pltpu.CompilerParams(dimension_semantics=("arbitrary", "arbitrary"),
                                             vmem_limit_bytes=VMEM_LIMIT),
        name="conv_module",
    )(glu4, glu4, w_row, w_col, conv_b, ln_g, ln_b)


def _ffn_kernel(x_p, x_m, x_n, yc_p, yc_m, yc_n, of_p, of_m, of_n, ob_p, ob_m, ob_n,
                      og_p, og_m, og_n, mod_ref, gn_ref, n2_ref, fg_ref, wout_ref, wup_ref,
                      dw_ref, dwb_ref, wdown_ref, out_ref, a_ref):
    i = pl.program_id(1)
    last = pl.num_programs(1) - 1
    tm = x_m.shape[1]
    main = slice(HALO, HALO + tm)

    def ext(p, m, n):
        return jnp.concatenate([p[0], m[0], n[0]], axis=0)

    mod = mod_ref[0]
    gate1 = mod[:, 2 * D_MODEL:3 * D_MODEL]
    shift2 = mod[:, 3 * D_MODEL:4 * D_MODEL]
    scale2 = mod[:, 4 * D_MODEL:5 * D_MODEL]
    gate2 = mod[:, 5 * D_MODEL:6 * D_MODEL]

    def ext16(p, m, n):
        return jnp.concatenate([p[0].astype(F32)[BF16_ROWS - HALO:], m[0].astype(F32),
                                n[0].astype(F32)[:HALO]], axis=0)

    o = ext16(of_p, of_m, of_n) + ext16(ob_p, ob_m, ob_n)
    og = ext(og_p, og_m, og_n)
    parts = [ext(yc_p, yc_m, yc_n).astype(BF16)]
    for h in range(GLA_HEADS):
        lanes = slice(h * GLA_HEAD_DV, (h + 1) * GLA_HEAD_DV)
        oh = o[:, lanes]
        ms = jnp.mean(oh * oh, axis=-1, keepdims=True)
        ogh = og[:, lanes]
        parts.append((oh * lax.rsqrt(ms + EPS) * gn_ref[:, lanes] * (ogh * _sigmoid(ogh))).astype(BF16))
    mix = jnp.concatenate(parts, axis=1)
    x1 = ext(x_p, x_m, x_n) + gate1 * _dot(mix, wout_ref[...])

    ms = jnp.mean(x1 * x1, axis=-1, keepdims=True)
    h2_f32 = (x1 * lax.rsqrt(ms + EPS) * n2_ref[...]) * (1.0 + scale2) + shift2
    h2 = h2_f32.astype(BF16)
    h2_main = h2_f32[main].astype(BF16)

    row = lax.broadcasted_iota(jnp.int32, (tm, 1), 0)
    kill_prev = jnp.logical_and(row == 0, i == 0)
    kill_next = jnp.logical_and(row == tm - 1, i == last)
    n_chunks = FFN_HIDDEN // FFN_CHUNK

    def up(j):
        a_ref[j % UP_SLOTS] = _dot(h2, wup_ref[:, j * FFN_CHUNK:(j + 1) * FFN_CHUNK])
        return _dot(h2_main, wup_ref[:, FFN_HIDDEN + j * FFN_CHUNK:FFN_HIDDEN + (j + 1) * FFN_CHUNK])

    acc = None
    vals = {j: up(j) for j in range(UP_SLOTS - 1)}
    pending = None
    for j in range(n_chunks):
        if j + UP_SLOTS - 1 < n_chunks:
            vals[j + UP_SLOTS - 1] = up(j + UP_SLOTS - 1)
        if pending is not None:
            part = _dot(pending[0], wdown_ref[pending[1], :])
            acc = part if acc is None else acc + part
        val = vals.pop(j)
        cols = slice(j * FFN_CHUNK, (j + 1) * FFN_CHUNK)
        a_cur = a_ref.at[j % UP_SLOTS]
        a_prev = jnp.where(kill_prev, 0.0, a_cur[HALO - 1:HALO - 1 + tm, :])
        a_next = jnp.where(kill_next, 0.0, a_cur[HALO + 1:HALO + 1 + tm, :])
        ac = (dw_ref[0:1, cols] * a_prev + dw_ref[1:2, cols] * a_cur[main, :]
              + dw_ref[2:3, cols] * a_next + dwb_ref[:, cols])
        f = (ac * _sigmoid(ac) * val).astype(BF16)
        pending = (f, cols)
    acc = acc + _dot(pending[0], wdown_ref[pending[1], :])

    x2 = x1[main] + gate2 * acc
    ms = jnp.mean(x2 * x2, axis=-1, keepdims=True)
    out_ref[0] = x2 * lax.rsqrt(ms + EPS) * fg_ref[...]


def _ffn(x, y_conv, o_f, o_b, og, mod, gn, n2, fg, w_out, w_up, dw, dwb, w_down):
    bsz, length, d = x.shape
    tm = FFN_TILE
    def trio(w, halo=HALO):
        per = tm // halo
        n_halo = length // halo
        return [pl.BlockSpec((1, halo, w), lambda b, i: (b, jnp.maximum(i * per - 1, 0), 0)),
                pl.BlockSpec((1, tm, w), lambda b, i: (b, i, 0)),
                pl.BlockSpec((1, halo, w), lambda b, i: (b, jnp.minimum((i + 1) * per, n_halo - 1), 0))]

    def const2(shape):
        return pl.BlockSpec(shape, lambda b, i: (0, 0))

    in_specs = (trio(d) + trio(CONV_W) + trio(GLA_DV, BF16_ROWS) + trio(GLA_DV, BF16_ROWS) + trio(GLA_DV)
                + [pl.BlockSpec((1, 1, N_MOD * d), lambda b, i: (b, 0, 0)),
                   const2((1, GLA_DV)), const2((1, d)), const2((1, d)),
                   const2((d, d)), const2((d, 2 * FFN_HIDDEN)),
                   const2((SUBLANES, FFN_HIDDEN)), const2((1, FFN_HIDDEN)),
                   const2((FFN_HIDDEN, d))])
    return pl.pallas_call(
        _ffn_kernel,
        grid=(bsz, length // tm),
        in_specs=in_specs,
        out_specs=pl.BlockSpec((1, tm, d), lambda b, i: (b, i, 0)),
        out_shape=jax.ShapeDtypeStruct((bsz, length, d), F32),
        scratch_shapes=[pltpu.VMEM((UP_SLOTS, tm + 2 * HALO, FFN_CHUNK), F32)],
        compiler_params=pltpu.CompilerParams(dimension_semantics=("arbitrary", "arbitrary"),
                                             vmem_limit_bytes=VMEM_LIMIT),
        name="out_proj_ffn",
    )(x, x, x, y_conv, y_conv, y_conv, o_f, o_f, o_f, o_b, o_b, o_b, og, og, og,
      mod, gn, n2, fg, w_out, w_up, dw, dwb, w_down)


def kernel(x, c, ctx, c_ctx, w_mod, b_mod, norm1_g, w_in, conv_dw, conv_b, conv_ln_g, conv_ln_b,
           w_gf, b_gf, w_gb, b_gb, gla_norm_g, w_out, norm2_g, w_up, ffn_dw, ffn_dw_b, w_down, final_g):
    bsz, seq, d = x.shape
    ctx_len = ctx.shape[1]
    layer = 0

    cvec = jnp.concatenate([c, c_ctx[None, :], jnp.zeros((SUBLANES - bsz - 1, d), F32)], axis=0)
    mods = _modulation(cvec, w_mod[layer], b_mod[layer][None, :])
    mod_lat = mods[:bsz][:, None, :]
    mod_ctx = jnp.broadcast_to(mods[bsz][None, None, :], (bsz, 1, N_MOD * d))

    assert w_in.shape[2] == D_IN_PAD
    w_in_p = w_in[layer]
    zeros_g = jnp.zeros((GATE_RANK, GLA_DK), F32)
    wgf = jnp.concatenate([w_gf[layer], zeros_g], axis=0).astype(BF16)
    wgb = jnp.concatenate([zeros_g, w_gb[layer]], axis=0).astype(BF16)
    bgf = b_gf[layer][None, :]
    bgb = b_gb[layer][None, :]
    n1 = norm1_g[layer][None, :]

    k_c, v_c, z_c = _input_proj(ctx, mod_ctx, n1, w_in_p, ctx_len, full=False)
    zero_state = jnp.zeros((bsz, GLA_HEADS, GLA_HEAD_DK, GLA_HEAD_DV), F32)
    s_f, s_b = _gla(None, k_c, v_c, z_c, wgf, bgf, wgb, bgb, zero_state, zero_state)

    glu, q, k, v, og, z = _input_proj(x, mod_lat, n1, w_in_p, IN_TILE)
    o_f, o_b, _, _ = _gla(q, k, v, z, wgf, bgf, wgb, bgb, s_f, s_b)

    dw = jnp.pad(conv_dw[layer], ((0, 4 * SUBLANES - CONV_K), (0, 0)))
    y_conv = _conv_module(glu.reshape(bsz, seq // GRID_W, GRID_W, CONV_W),
                          dw[:, :CONV_HALF], dw[:, CONV_HALF:], conv_b[layer][None, :],
                          conv_ln_g[layer][None, :], conv_ln_b[layer][None, :])
    y_conv = y_conv.reshape(bsz, seq, CONV_W)

    ffn_w = jnp.pad(ffn_dw[layer], ((0, SUBLANES - ffn_dw.shape[1]), (0, 0)))
    return _ffn(x, y_conv, o_f, o_b, og, mod_lat, gla_norm_g[layer][None, :], norm2_g[layer][None, :],
                      final_g[None, :], w_out[layer].astype(BF16), w_up[layer].astype(BF16),
                      ffn_w, ffn_dw_b[layer][None, :], w_down[layer].astype(BF16))
```

```python
import functools

import jax
import jax.numpy as jnp
from jax import lax
from jax.experimental import pallas as pl
from jax.experimental.pallas import tpu as pltpu

F32 = jnp.float32
BF16 = jnp.bfloat16

D_MODEL = 1024
GRID_W = 64
CONV_W = 512
CONV_HALF = CONV_W // 2
CONV_K = 31
CONV_PAD = CONV_K // 2
GLA_HEADS = 4
GLA_DV = 512
GLA_HEAD_DV = 128
GLA_DK = 256
GLA_HEAD_DK = 64
GATE_RANK = 16
GATE_NORM = 16.0
CHUNK = 64
FFN_HIDDEN = 2816
N_MOD = 6
EPS = 1e-6

LANES = 128
SUBLANES = 8
V7X_VMEM_BYTES = 64 * 1024 * 1024
VMEM_LIMIT = V7X_VMEM_BYTES - 8 * 1024 * 1024

Z_PAD = 2 * GATE_RANK
D_IN_PAD = 2 * CONV_W + 2 * GLA_DK + 2 * GLA_DV + Z_PAD
OFF_CU, OFF_CG = 0, CONV_W
OFF_Q = 2 * CONV_W
OFF_K = OFF_Q + GLA_DK
OFF_V = OFF_K + GLA_DK
OFF_OG = OFF_V + GLA_DV
OFF_Z = OFF_OG + GLA_DV

IN_TILE = 1024
IN_SUB = 256
GLA_BLOCK = 2048
CONV_ROWS = 16
FFN_TILE = 512
FFN_CHUNK = 256
HALO = SUBLANES
UP_SLOTS = 4


def _sigmoid(x):
    return 1.0 / (1.0 + jnp.exp(-x))


def _split_bf16(x):
    hi = x.astype(BF16)
    lo = (x - hi.astype(F32)).astype(BF16)
    return hi, lo


def _dot(a, b):
    return jnp.dot(a, b, preferred_element_type=F32)


def _mod_kernel(c_ref, w_ref, b_ref, o_ref):
    c = c_ref[...]
    s = c * _sigmoid(c)
    s_hi, s_lo = _split_bf16(s)
    w = w_ref[...].astype(BF16)
    o_ref[...] = _dot(s_hi, w) + _dot(s_lo, w) + b_ref[...]


def _modulation(cvec, w_mod, b_mod):
    rows, d = cvec.shape
    n = w_mod.shape[1]
    bn = D_MODEL
    return pl.pallas_call(
        _mod_kernel,
        grid=(n // bn,),
        in_specs=[pl.BlockSpec((rows, d), lambda j: (0, 0)),
                  pl.BlockSpec((d, bn), lambda j: (0, j)),
                  pl.BlockSpec((1, bn), lambda j: (0, j))],
        out_specs=pl.BlockSpec((rows, bn), lambda j: (0, j)),
        out_shape=jax.ShapeDtypeStruct((rows, n), F32),
        compiler_params=pltpu.CompilerParams(dimension_semantics=("arbitrary",),
                                             vmem_limit_bytes=VMEM_LIMIT),
        name="modulation",
    )(cvec, w_mod, b_mod)


def _in_kernel(x_ref, mod_ref, g_ref, w_ref, *out_refs, full):
    m = mod_ref[0]
    shift = m[:, 0:D_MODEL]
    scale = m[:, D_MODEL:2 * D_MODEL]
    tm = x_ref.shape[1]
    sub = min(tm, IN_SUB)
    n_sub = tm // sub

    hs = []
    for s in range(n_sub):
        x = x_ref[0, s * sub:(s + 1) * sub, :]
        ms = jnp.mean(x * x, axis=-1, keepdims=True)
        y = x * lax.rsqrt(ms + EPS) * g_ref[...]
        hs.append((y * (1.0 + scale) + shift).astype(BF16))

    groups = ([(OFF_CU, CONV_W), (OFF_CG, CONV_W), (OFF_Q, GLA_DK), (OFF_OG, GLA_DV)] if full else []) + [
        (OFF_K, GLA_DK), (OFF_V, GLA_DV), (OFF_Z, Z_PAD)]
    w_bf = {lo: w_ref[:, lo:lo + width].astype(BF16) for lo, width in groups}

    for s in range(n_sub):
        rows = slice(s * sub, (s + 1) * sub)

        def proj(lo, width):
            return _dot(hs[s], w_bf[lo])

        if full:
            glu_ref, q_ref, k_ref, v_ref, og_ref, z_ref = out_refs
            cu, cg, q_val = proj(OFF_CU, CONV_W), proj(OFF_CG, CONV_W), proj(OFF_Q, GLA_DK)
            glu_ref[0, rows, :] = (cu * _sigmoid(cg)).astype(glu_ref.dtype)
            q_ref[0, rows, :] = q_val.astype(q_ref.dtype)
            og_ref[0, rows, :] = proj(OFF_OG, GLA_DV).astype(og_ref.dtype)
        else:
            k_ref, v_ref, z_ref = out_refs
        k_ref[0, rows, :] = proj(OFF_K, GLA_DK).astype(k_ref.dtype)
        v_ref[0, rows, :] = proj(OFF_V, GLA_DV).astype(v_ref.dtype)
        z_ref[0, rows, :] = proj(OFF_Z, Z_PAD).astype(z_ref.dtype)


def _input_proj(x, mod, norm_g, w_in_p, tm, full=True):
    bsz, length, d = x.shape
    if full:
        widths = (CONV_W, GLA_DK, GLA_DK, GLA_DV, GLA_DV, Z_PAD)
        dtypes = (BF16, BF16, BF16, BF16, F32, BF16)
    else:
        widths = (GLA_DK, GLA_DV, Z_PAD)
        dtypes = (BF16, BF16, BF16)
    return pl.pallas_call(
        functools.partial(_in_kernel, full=full),
        grid=(bsz, length // tm),
        in_specs=[pl.BlockSpec((1, tm, d), lambda b, i: (b, i, 0)),
                  pl.BlockSpec((1, 1, N_MOD * d), lambda b, i: (b, 0, 0)),
                  pl.BlockSpec((1, d), lambda b, i: (0, 0)),
                  pl.BlockSpec((d, D_IN_PAD), lambda b, i: (0, 0))],
        out_specs=[pl.BlockSpec((1, tm, w), lambda b, i: (b, i, 0)) for w in widths],
        out_shape=[jax.ShapeDtypeStruct((bsz, length, w), dt) for w, dt in zip(widths, dtypes)],
        compiler_params=pltpu.CompilerParams(dimension_semantics=("arbitrary", "arbitrary"),
                                             vmem_limit_bytes=VMEM_LIMIT),
        name="input_proj",
    )(x, mod, norm_g, w_in_p)


def _gla_block(q_ref, k_ref, v_ref, z_ref, wg, bg, s_ref, o_ref, reverse):
    n_chunks = k_ref.shape[1] // CHUNK
    row = lax.broadcasted_iota(jnp.int32, (CHUNK, CHUNK), 0)
    col = lax.broadcasted_iota(jnp.int32, (CHUNK, CHUNK), 1)
    tri = ((col >= row) if reverse else (col <= row)).astype(BF16)
    row2 = lax.broadcasted_iota(jnp.int32, (CHUNK, LANES), 0)
    lane = lax.broadcasted_iota(jnp.int32, (CHUNK, LANES), 1)
    col2 = lane & (CHUNK - 1)
    keep_pair = (col2 >= row2) if reverse else (col2 <= row2)
    first = lane < GLA_HEAD_DK
    zero = jnp.zeros((CHUNK, LANES), BF16)

    pre = _dot(z_ref[0, :, 0:2 * GATE_RANK].astype(BF16), wg) + bg
    g = (jnp.minimum(pre, 0.0) - jnp.log1p(jnp.exp(-jnp.abs(pre)))) * (1.0 / GATE_NORM)
    g_hi, g_lo = _split_bf16(g)
    b_parts, last_parts, decays = [], [], []
    for c in range(n_chunks):
        rows = slice(c * CHUNK, (c + 1) * CHUNK)
        b_c = _dot(tri, g_hi[rows]) + _dot(tri, g_lo[rows])
        b_last = b_c[0:1] if reverse else b_c[CHUNK - 1:CHUNK]
        b_parts.append(b_c)
        last_parts.append(jnp.broadcast_to(b_last, (CHUNK, GLA_DK)))
        decays.append(jnp.broadcast_to(jnp.exp(b_last), (LANES, GLA_DK)).T)
    b = jnp.concatenate(b_parts, axis=0)
    need_out = o_ref is not None
    k = k_ref[0].astype(F32)
    if need_out:
        q_e = (q_ref[0].astype(F32) * (GLA_HEAD_DK ** -0.5) * jnp.exp(b)).astype(BF16)
        k_e = k * jnp.exp(-b)
    k_t = (k * jnp.exp(jnp.concatenate(last_parts, axis=0) - b)).astype(BF16)
    v = v_ref[0].astype(BF16)

    o_intra, kvs = [], []
    for c in range(n_chunks):
        rows = slice(c * CHUNK, (c + 1) * CHUNK)
        o_c, kv_c = [], []
        for p in range(GLA_HEADS // 2):
            lanes = slice(p * LANES, (p + 1) * LANES)
            wide = slice(2 * p * LANES, 2 * (p + 1) * LANES)
            vp = v[rows, wide]
            if need_out:
                kg = k_e[rows, lanes]
                k_blk = jnp.concatenate([jnp.where(first, kg, 0.0), jnp.where(first, 0.0, kg)],
                                        axis=0).astype(BF16)
                scores = lax.dot_general(q_e[rows, lanes], k_blk, (((1,), (1,)), ((), ())),
                                         preferred_element_type=F32)
                a = jnp.where(keep_pair, scores, 0.0).astype(BF16)
                v_blk = jnp.concatenate(
                    [jnp.concatenate([vp[:, :LANES], zero], axis=1),
                     jnp.concatenate([zero, vp[:, LANES:]], axis=1)], axis=0)
                o_c.append((a, v_blk))
            kv_c.append(lax.dot_general(k_t[rows, lanes], vp,
                                        (((0,), (0,)), ((), ())), preferred_element_type=F32))
        o_intra.append(o_c)
        kvs.append(kv_c)

    state = [s_ref[h] for h in range(GLA_HEADS)]
    zero_s = jnp.zeros((GLA_HEAD_DK, GLA_HEAD_DV), F32)
    h0 = slice(0, GLA_HEAD_DK)
    h1 = slice(GLA_HEAD_DK, 2 * GLA_HEAD_DK)
    for c in (range(n_chunks - 1, -1, -1) if reverse else range(n_chunks)):
        rows = slice(c * CHUNK, (c + 1) * CHUNK)
        for p in range(GLA_HEADS // 2):
            lanes = slice(p * LANES, (p + 1) * LANES)
            wide = slice(2 * p * LANES, 2 * (p + 1) * LANES)
            if need_out:
                s_blk = jnp.concatenate(
                    [jnp.concatenate([state[2 * p], zero_s], axis=1),
                     jnp.concatenate([zero_s, state[2 * p + 1]], axis=1)], axis=0).astype(BF16)
                a, v_blk = o_intra[c][p]
                o_ref[0, rows, wide] = _dot(jnp.concatenate([q_e[rows, lanes], a], axis=1),
                                            jnp.concatenate([s_blk, v_blk], axis=0)).astype(o_ref.dtype)
            dp = decays[c][lanes]
            kv = kvs[c][p]
            state[2 * p] = dp[h0] * state[2 * p] + kv[h0, 0:LANES]
            state[2 * p + 1] = dp[h1] * state[2 * p + 1] + kv[h1, LANES:2 * LANES]
    for h in range(GLA_HEADS):
        s_ref[h] = state[h]


def _gla_kernel(*refs, need_out):
    if need_out:
        (qf_ref, kf_ref, vf_ref, zf_ref, qb_ref, kb_ref, vb_ref, zb_ref,
         wgf_ref, bgf_ref, wgb_ref, bgb_ref, sf0_ref, sb0_ref,
         of_ref, ob_ref, sf_out_ref, sb_out_ref, sf_ref, sb_ref) = refs
    else:
        (kf_ref, vf_ref, zf_ref, kb_ref, vb_ref, zb_ref,
         wgf_ref, bgf_ref, wgb_ref, bgb_ref, sf0_ref, sb0_ref,
         sf_out_ref, sb_out_ref, sf_ref, sb_ref) = refs
        qf_ref = qb_ref = of_ref = ob_ref = None
    i = pl.program_id(1)

    @pl.when(i == 0)
    def _():
        sf_ref[...] = sf0_ref[0]
        sb_ref[...] = sb0_ref[0]

    _gla_block(qf_ref, kf_ref, vf_ref, zf_ref, wgf_ref[...], bgf_ref[...], sf_ref, of_ref, False)
    _gla_block(qb_ref, kb_ref, vb_ref, zb_ref, wgb_ref[...], bgb_ref[...], sb_ref, ob_ref, True)

    @pl.when(i == pl.num_programs(1) - 1)
    def _():
        sf_out_ref[0] = sf_ref[...]
        sb_out_ref[0] = sb_ref[...]


def _gla(q, k, v, z, wgf, bgf, wgb, bgb, sf0, sb0):
    need_out = q is not None
    bsz, length, _ = k.shape
    tb = min(GLA_BLOCK, length)
    assert length % tb == 0 and tb % CHUNK == 0
    nb = length // tb

    def fwd(w):
        return pl.BlockSpec((1, tb, w), lambda b, i: (b, i, 0))

    def bwd(w):
        return pl.BlockSpec((1, tb, w), lambda b, i: (b, nb - 1 - i, 0))

    def const2(shape):
        return pl.BlockSpec(shape, lambda b, i: (0, 0))

    state = pl.BlockSpec((1, GLA_HEADS, GLA_HEAD_DK, GLA_HEAD_DV), lambda b, i: (b, 0, 0, 0))
    state_shape = jax.ShapeDtypeStruct((bsz, GLA_HEADS, GLA_HEAD_DK, GLA_HEAD_DV), F32)
    o_shape = jax.ShapeDtypeStruct((bsz, length, GLA_DV), F32)
    q_f = [fwd(GLA_DK)] if need_out else []
    q_b = [bwd(GLA_DK)] if need_out else []
    q_arg = [q] if need_out else []
    return pl.pallas_call(
        functools.partial(_gla_kernel, need_out=need_out),
        grid=(bsz, nb),
        in_specs=(q_f + [fwd(GLA_DK), fwd(GLA_DV), fwd(Z_PAD)]
                  + q_b + [bwd(GLA_DK), bwd(GLA_DV), bwd(Z_PAD)]
                  + [const2((2 * GATE_RANK, GLA_DK)), const2((1, GLA_DK)),
                     const2((2 * GATE_RANK, GLA_DK)), const2((1, GLA_DK)),
                     state, state]),
        out_specs=([fwd(GLA_DV), bwd(GLA_DV)] if need_out else []) + [state, state],
        out_shape=([o_shape, o_shape] if need_out else []) + [state_shape, state_shape],
        scratch_shapes=[pltpu.VMEM((GLA_HEADS, GLA_HEAD_DK, GLA_HEAD_DV), F32),
                        pltpu.VMEM((GLA_HEADS, GLA_HEAD_DK, GLA_HEAD_DV), F32)],
        compiler_params=pltpu.CompilerParams(dimension_semantics=("arbitrary", "arbitrary"),
                                             vmem_limit_bytes=VMEM_LIMIT),
        name="gla",
    )(*q_arg, k, v, z, *q_arg, k, v, z, wgf, bgf, wgb, bgb, sf0, sb0)


def _conv_kernel(grow_ref, gcol_ref, wrow_ref, wcol_ref, cb_ref, lg_ref, lb_ref, y_ref,
                 rowpad_ref, colpad_ref, yrow_ref):
    t = pl.program_id(1)
    n_rows = gcol_ref.shape[1]
    pad_rows = CONV_ROWS
    pad_w = 2 * SUBLANES

    @pl.when(t == 0)
    def _():
        rowpad_ref[...] = jnp.zeros(rowpad_ref.shape, F32)
        zeros = jnp.zeros((pad_rows, GRID_W, CONV_HALF), F32)
        colpad_ref[0:pad_rows] = zeros
        colpad_ref[pad_rows + n_rows:pad_rows + n_rows + pad_rows] = zeros

        def copy(j, carry):
            colpad_ref[pl.ds(pad_rows + j * CONV_ROWS, CONV_ROWS)] = (
                gcol_ref[0, pl.ds(j * CONV_ROWS, CONV_ROWS)].astype(F32))
            return carry
        lax.fori_loop(0, n_rows // CONV_ROWS, copy, 0)

    for lg in range(CONV_HALF // LANES):
        rowpad_ref[lg, :, pad_w:pad_w + GRID_W, :] = grow_ref[0, :, :, lg * LANES:(lg + 1) * LANES].astype(F32)

    half_w = GRID_W // 2

    def row_body(r, carry):
        pieces = []
        for lg in range(CONV_HALF // LANES):
            lanes = slice(lg * LANES, (lg + 1) * LANES)
            for parity in range(2):
                acc = jnp.broadcast_to(cb_ref[:, lanes], (half_w, LANES))
                for kk in range(CONV_K):
                    start = pad_w + kk - CONV_PAD + parity
                    acc = acc + (wrow_ref[kk:kk + 1, lanes]
                                 * rowpad_ref[lg, r, pl.ds(start, half_w, stride=2), :])
                yrow_ref[lg, pl.ds(parity, half_w, stride=2), :] = acc
            pieces.append(yrow_ref[lg])
        for lg in range(CONV_HALF // LANES):
            lanes = slice(lg * LANES, (lg + 1) * LANES)
            acc = jnp.broadcast_to(cb_ref[:, CONV_HALF + lg * LANES:CONV_HALF + (lg + 1) * LANES],
                                   (GRID_W, LANES))
            for kk in range(CONV_K):
                src = t * CONV_ROWS + r + (pad_rows + kk - CONV_PAD)
                acc = acc + wcol_ref[kk:kk + 1, lanes] * colpad_ref[src, :, lanes]
            pieces.append(acc)
        y = jnp.concatenate(pieces, axis=1)
        mu = jnp.mean(y, axis=-1, keepdims=True)
        var = jnp.mean(y * y, axis=-1, keepdims=True) - mu * mu
        yn = (y - mu) * lax.rsqrt(var + EPS) * lg_ref[...] + lb_ref[...]
        y_ref[0, r] = (yn * _sigmoid(yn)).astype(y_ref.dtype)
        return carry

    lax.fori_loop(0, CONV_ROWS, row_body, 0, unroll=2)


def _conv_module(glu4, w_row, w_col, conv_b, ln_g, ln_b):
    bsz, n_rows, gw, _ = glu4.shape
    return pl.pallas_call(
        _conv_kernel,
        grid=(bsz, n_rows // CONV_ROWS),
        in_specs=[pl.BlockSpec((1, CONV_ROWS, gw, CONV_HALF), lambda b, t: (b, t, 0, 0)),
                  pl.BlockSpec((1, n_rows, gw, CONV_HALF), lambda b, t: (b, 0, 0, 1)),
                  pl.BlockSpec((4 * SUBLANES, CONV_HALF), lambda b, t: (0, 0)),
                  pl.BlockSpec((4 * SUBLANES, CONV_HALF), lambda b, t: (0, 0)),
                  pl.BlockSpec((1, CONV_W), lambda b, t: (0, 0)),
                  pl.BlockSpec((1, CONV_W), lambda b, t: (0, 0)),
                  pl.BlockSpec((1, CONV_W), lambda b, t: (0, 0))],
        out_specs=pl.BlockSpec((1, CONV_ROWS, gw, CONV_W), lambda b, t: (b, t, 0, 0)),
        out_shape=jax.ShapeDtypeStruct((bsz, n_rows, gw, CONV_W), F32),
        scratch_shapes=[pltpu.VMEM((CONV_HALF // LANES, CONV_ROWS, gw + 4 * SUBLANES, LANES), F32),
                        pltpu.VMEM((n_rows + 2 * CONV_ROWS, gw, CONV_HALF), F32),
                        pltpu.VMEM((CONV_HALF // LANES, gw, LANES), F32)],
        compiler_params=pltpu.CompilerParams(dimension_semantics=("arbitrary", "arbitrary"),
                                             vmem_limit_bytes=VMEM_LIMIT),
        name="conv_module",
    )(glu4, glu4, w_row, w_col, conv_b, ln_g, ln_b)


def _ffn_kernel(x_p, x_m, x_n, yc_p, yc_m, yc_n, of_p, of_m, of_n, ob_p, ob_m, ob_n,
                      og_p, og_m, og_n, mod_ref, gn_ref, n2_ref, fg_ref, wout_ref, wup_ref,
                      dw_ref, dwb_ref, wdown_ref, out_ref, a_ref):
    i = pl.program_id(1)
    last = pl.num_programs(1) - 1
    tm = x_m.shape[1]
    main = slice(HALO, HALO + tm)

    def ext(p, m, n):
        return jnp.concatenate([p[0], m[0], n[0]], axis=0)

    mod = mod_ref[0]
    gate1 = mod[:, 2 * D_MODEL:3 * D_MODEL]
    shift2 = mod[:, 3 * D_MODEL:4 * D_MODEL]
    scale2 = mod[:, 4 * D_MODEL:5 * D_MODEL]
    gate2 = mod[:, 5 * D_MODEL:6 * D_MODEL]

    o = ext(of_p, of_m, of_n) + ext(ob_p, ob_m, ob_n)
    og = ext(og_p, og_m, og_n)
    parts = [ext(yc_p, yc_m, yc_n).astype(BF16)]
    for h in range(GLA_HEADS):
        lanes = slice(h * GLA_HEAD_DV, (h + 1) * GLA_HEAD_DV)
        oh = o[:, lanes]
        ms = jnp.mean(oh * oh, axis=-1, keepdims=True)
        ogh = og[:, lanes]
        parts.append((oh * lax.rsqrt(ms + EPS) * gn_ref[:, lanes] * (ogh * _sigmoid(ogh))).astype(BF16))
    mix = jnp.concatenate(parts, axis=1)
    x1 = ext(x_p, x_m, x_n) + gate1 * _dot(mix, wout_ref[...])

    ms = jnp.mean(x1 * x1, axis=-1, keepdims=True)
    h2_f32 = (x1 * lax.rsqrt(ms + EPS) * n2_ref[...]) * (1.0 + scale2) + shift2
    h2 = h2_f32.astype(BF16)
    h2_main = h2_f32[main].astype(BF16)

    row = lax.broadcasted_iota(jnp.int32, (tm, 1), 0)
    kill_prev = jnp.logical_and(row == 0, i == 0)
    kill_next = jnp.logical_and(row == tm - 1, i == last)
    n_chunks = FFN_HIDDEN // FFN_CHUNK

    def up(j):
        a_ref[j % UP_SLOTS] = _dot(h2, wup_ref[:, j * FFN_CHUNK:(j + 1) * FFN_CHUNK])
        return _dot(h2_main, wup_ref[:, FFN_HIDDEN + j * FFN_CHUNK:FFN_HIDDEN + (j + 1) * FFN_CHUNK])

    acc = None
    vals = {j: up(j) for j in range(UP_SLOTS - 1)}
    pending = None
    for j in range(n_chunks):
        if j + UP_SLOTS - 1 < n_chunks:
            vals[j + UP_SLOTS - 1] = up(j + UP_SLOTS - 1)
        if pending is not None:
            part = _dot(pending[0], wdown_ref[pending[1], :])
            acc = part if acc is None else acc + part
        val = vals.pop(j)
        cols = slice(j * FFN_CHUNK, (j + 1) * FFN_CHUNK)
        a_cur = a_ref.at[j % UP_SLOTS]
        a_prev = jnp.where(kill_prev, 0.0, a_cur[HALO - 1:HALO - 1 + tm, :])
        a_next = jnp.where(kill_next, 0.0, a_cur[HALO + 1:HALO + 1 + tm, :])
        ac = (dw_ref[0:1, cols] * a_prev + dw_ref[1:2, cols] * a_cur[main, :]
              + dw_ref[2:3, cols] * a_next + dwb_ref[:, cols])
        f = (ac * _sigmoid(ac) * val).astype(BF16)
        pending = (f, cols)
    acc = acc + _dot(pending[0], wdown_ref[pending[1], :])

    x2 = x1[main] + gate2 * acc
    ms = jnp.mean(x2 * x2, axis=-1, keepdims=True)
    out_ref[0] = x2 * lax.rsqrt(ms + EPS) * fg_ref[...]


def _ffn(x, y_conv, o_f, o_b, og, mod, gn, n2, fg, w_out, w_up, dw, dwb, w_down):
    bsz, length, d = x.shape
    tm = FFN_TILE
    def trio(w, halo=HALO):
        per = tm // halo
        n_halo = length // halo
        return [pl.BlockSpec((1, halo, w), lambda b, i: (b, jnp.maximum(i * per - 1, 0), 0)),
                pl.BlockSpec((1, tm, w), lambda b, i: (b, i, 0)),
                pl.BlockSpec((1, halo, w), lambda b, i: (b, jnp.minimum((i + 1) * per, n_halo - 1), 0))]

    def const2(shape):
        return pl.BlockSpec(shape, lambda b, i: (0, 0))

    in_specs = (trio(d) + trio(CONV_W) + trio(GLA_DV) + trio(GLA_DV) + trio(GLA_DV)
                + [pl.BlockSpec((1, 1, N_MOD * d), lambda b, i: (b, 0, 0)),
                   const2((1, GLA_DV)), const2((1, d)), const2((1, d)),
                   const2((d, d)), const2((d, 2 * FFN_HIDDEN)),
                   const2((SUBLANES, FFN_HIDDEN)), const2((1, FFN_HIDDEN)),
                   const2((FFN_HIDDEN, d))])
    return pl.pallas_call(
        _ffn_kernel,
        grid=(bsz, length // tm),
        in_specs=in_specs,
        out_specs=pl.BlockSpec((1, tm, d), lambda b, i: (b, i, 0)),
        out_shape=jax.ShapeDtypeStruct((bsz, length, d), F32),
        scratch_shapes=[pltpu.VMEM((UP_SLOTS, tm + 2 * HALO, FFN_CHUNK), F32)],
        compiler_params=pltpu.CompilerParams(dimension_semantics=("arbitrary", "arbitrary"),
                                             vmem_limit_bytes=VMEM_LIMIT),
        name="out_proj_ffn",
    )(x, x, x, y_conv, y_conv, y_conv, o_f, o_f, o_f, o_b, o_b, o_b, og, og, og,
      mod, gn, n2, fg, w_out, w_up, dw, dwb, w_down)


def kernel(x, c, ctx, c_ctx, w_mod, b_mod, norm1_g, w_in, conv_dw, conv_b, conv_ln_g, conv_ln_b,
           w_gf, b_gf, w_gb, b_gb, gla_norm_g, w_out, norm2_g, w_up, ffn_dw, ffn_dw_b, w_down, final_g):
    bsz, seq, d = x.shape
    ctx_len = ctx.shape[1]
    layer = 0

    cvec = jnp.concatenate([c, c_ctx[None, :], jnp.zeros((SUBLANES - bsz - 1, d), F32)], axis=0)
    mods = _modulation(cvec, w_mod[layer], b_mod[layer][None, :])
    mod_lat = mods[:bsz][:, None, :]
    mod_ctx = jnp.broadcast_to(mods[bsz][None, None, :], (bsz, 1, N_MOD * d))

    assert w_in.shape[2] == D_IN_PAD
    w_in_p = w_in[layer]
    zeros_g = jnp.zeros((GATE_RANK, GLA_DK), F32)
    wgf = jnp.concatenate([w_gf[layer], zeros_g], axis=0).astype(BF16)
    wgb = jnp.concatenate([zeros_g, w_gb[layer]], axis=0).astype(BF16)
    bgf = b_gf[layer][None, :]
    bgb = b_gb[layer][None, :]
    n1 = norm1_g[layer][None, :]

    k_c, v_c, z_c = _input_proj(ctx, mod_ctx, n1, w_in_p, ctx_len, full=False)
    zero_state = jnp.zeros((bsz, GLA_HEADS, GLA_HEAD_DK, GLA_HEAD_DV), F32)
    s_f, s_b = _gla(None, k_c, v_c, z_c, wgf, bgf, wgb, bgb, zero_state, zero_state)

    glu, q, k, v, og, z = _input_proj(x, mod_lat, n1, w_in_p, IN_TILE)
    o_f, o_b, _, _ = _gla(q, k, v, z, wgf, bgf, wgb, bgb, s_f, s_b)

    dw = jnp.pad(conv_dw[layer], ((0, 4 * SUBLANES - CONV_K), (0, 0)))
    y_conv = _conv_module(glu.reshape(bsz, seq // GRID_W, GRID_W, CONV_W),
                          dw[:, :CONV_HALF], dw[:, CONV_HALF:], conv_b[layer][None, :],
                          conv_ln_g[layer][None, :], conv_ln_b[layer][None, :])
    y_conv = y_conv.reshape(bsz, seq, CONV_W)

    ffn_w = jnp.pad(ffn_dw[layer], ((0, SUBLANES - ffn_dw.shape[1]), (0, 0)))
    return _ffn(x, y_conv, o_f, o_b, og, mod_lat, gla_norm_g[layer][None, :], norm2_g[layer][None, :],
                      final_g[None, :], w_out[layer].astype(BF16), w_up[layer].astype(BF16),
                      ffn_w, ffn_dw_b[layer][None, :], w_down[layer].astype(BF16))
```

```python
import functools

import jax
import jax.numpy as jnp
from jax import lax
from jax.experimental import pallas as pl
from jax.experimental.pallas import tpu as pltpu

F32 = jnp.float32
BF16 = jnp.bfloat16

D_MODEL = 1024
GRID_W = 64
CONV_W = 512
CONV_HALF = CONV_W // 2
CONV_K = 31
CONV_PAD = CONV_K // 2
GLA_HEADS = 4
GLA_DV = 512
GLA_HEAD_DV = 128
GLA_DK = 256
GLA_HEAD_DK = 64
GATE_RANK = 16
GATE_NORM = 16.0
CHUNK = 64
FFN_HIDDEN = 2816
N_MOD = 6
EPS = 1e-6

LANES = 128
SUBLANES = 8
V7X_VMEM_BYTES = 64 * 1024 * 1024
VMEM_LIMIT = V7X_VMEM_BYTES - 8 * 1024 * 1024

Z_PAD = 2 * GATE_RANK
D_IN_PAD = 2 * CONV_W + 2 * GLA_DK + 2 * GLA_DV + Z_PAD
OFF_CU, OFF_CG = 0, CONV_W
OFF_Q = 2 * CONV_W
OFF_K = OFF_Q + GLA_DK
OFF_V = OFF_K + GLA_DK
OFF_OG = OFF_V + GLA_DV
OFF_Z = OFF_OG + GLA_DV

IN_TILE = 1024
IN_SUB = 256
GLA_BLOCK = 2048
CONV_ROWS = 16
FFN_TILE = 512
FFN_CHUNK = 256
HALO = SUBLANES
UP_SLOTS = 6


def _sigmoid(x):
    return 1.0 / (1.0 + jnp.exp(-x))


def _split_bf16(x):
    hi = x.astype(BF16)
    lo = (x - hi.astype(F32)).astype(BF16)
    return hi, lo


def _dot(a, b):
    return jnp.dot(a, b, preferred_element_type=F32)


def _mod_kernel(c_ref, w_ref, b_ref, o_ref):
    c = c_ref[...]
    s = c * _sigmoid(c)
    s_hi, s_lo = _split_bf16(s)
    w = w_ref[...].astype(BF16)
    o_ref[...] = _dot(s_hi, w) + _dot(s_lo, w) + b_ref[...]


def _modulation(cvec, w_mod, b_mod):
    rows, d = cvec.shape
    n = w_mod.shape[1]
    bn = D_MODEL
    return pl.pallas_call(
        _mod_kernel,
        grid=(n // bn,),
        in_specs=[pl.BlockSpec((rows, d), lambda j: (0, 0)),
                  pl.BlockSpec((d, bn), lambda j: (0, j)),
                  pl.BlockSpec((1, bn), lambda j: (0, j))],
        out_specs=pl.BlockSpec((rows, bn), lambda j: (0, j)),
        out_shape=jax.ShapeDtypeStruct((rows, n), F32),
        compiler_params=pltpu.CompilerParams(dimension_semantics=("arbitrary",),
                                             vmem_limit_bytes=VMEM_LIMIT),
        name="modulation",
    )(cvec, w_mod, b_mod)


def _in_kernel(x_ref, mod_ref, g_ref, w_ref, *out_refs, full):
    m = mod_ref[0]
    shift = m[:, 0:D_MODEL]
    scale = m[:, D_MODEL:2 * D_MODEL]
    tm = x_ref.shape[1]
    sub = min(tm, IN_SUB)
    n_sub = tm // sub

    hs = []
    for s in range(n_sub):
        x = x_ref[0, s * sub:(s + 1) * sub, :]
        ms = jnp.mean(x * x, axis=-1, keepdims=True)
        y = x * lax.rsqrt(ms + EPS) * g_ref[...]
        hs.append((y * (1.0 + scale) + shift).astype(BF16))

    groups = ([(OFF_CU, CONV_W), (OFF_CG, CONV_W), (OFF_Q, GLA_DK), (OFF_OG, GLA_DV)] if full else []) + [
        (OFF_K, GLA_DK), (OFF_V, GLA_DV), (OFF_Z, Z_PAD)]
    w_bf = {lo: w_ref[:, lo:lo + width].astype(BF16) for lo, width in groups}

    for s in range(n_sub):
        rows = slice(s * sub, (s + 1) * sub)

        def proj(lo, width):
            return _dot(hs[s], w_bf[lo])

        if full:
            glu_ref, q_ref, k_ref, v_ref, og_ref, z_ref = out_refs
            cu, cg, q_val = proj(OFF_CU, CONV_W), proj(OFF_CG, CONV_W), proj(OFF_Q, GLA_DK)
            glu_ref[0, rows, :] = (cu * _sigmoid(cg)).astype(glu_ref.dtype)
            q_ref[0, rows, :] = q_val.astype(q_ref.dtype)
            og_ref[0, rows, :] = proj(OFF_OG, GLA_DV).astype(og_ref.dtype)
        else:
            k_ref, v_ref, z_ref = out_refs
        k_ref[0, rows, :] = proj(OFF_K, GLA_DK).astype(k_ref.dtype)
        v_ref[0, rows, :] = proj(OFF_V, GLA_DV).astype(v_ref.dtype)
        z_ref[0, rows, :] = proj(OFF_Z, Z_PAD).astype(z_ref.dtype)


def _input_proj(x, mod, norm_g, w_in_p, tm, full=True):
    bsz, length, d = x.shape
    if full:
        widths = (CONV_W, GLA_DK, GLA_DK, GLA_DV, GLA_DV, Z_PAD)
        dtypes = (BF16, BF16, BF16, BF16, F32, BF16)
    else:
        widths = (GLA_DK, GLA_DV, Z_PAD)
        dtypes = (BF16, BF16, BF16)
    return pl.pallas_call(
        functools.partial(_in_kernel, full=full),
        grid=(bsz, length // tm),
        in_specs=[pl.BlockSpec((1, tm, d), lambda b, i: (b, i, 0)),
                  pl.BlockSpec((1, 1, N_MOD * d), lambda b, i: (b, 0, 0)),
                  pl.BlockSpec((1, d), lambda b, i: (0, 0)),
                  pl.BlockSpec((d, D_IN_PAD), lambda b, i: (0, 0))],
        out_specs=[pl.BlockSpec((1, tm, w), lambda b, i: (b, i, 0)) for w in widths],
        out_shape=[jax.ShapeDtypeStruct((bsz, length, w), dt) for w, dt in zip(widths, dtypes)],
        compiler_params=pltpu.CompilerParams(dimension_semantics=("arbitrary", "arbitrary"),
                                             vmem_limit_bytes=VMEM_LIMIT),
        name="input_proj",
    )(x, mod, norm_g, w_in_p)


def _gla_block(q_ref, k_ref, v_ref, z_ref, wg, bg, s_ref, o_ref, reverse):
    n_chunks = k_ref.shape[1] // CHUNK
    row = lax.broadcasted_iota(jnp.int32, (CHUNK, CHUNK), 0)
    col = lax.broadcasted_iota(jnp.int32, (CHUNK, CHUNK), 1)
    tri = ((col >= row) if reverse else (col <= row)).astype(BF16)
    row2 = lax.broadcasted_iota(jnp.int32, (CHUNK, LANES), 0)
    lane = lax.broadcasted_iota(jnp.int32, (CHUNK, LANES), 1)
    col2 = lane & (CHUNK - 1)
    keep_pair = (col2 >= row2) if reverse else (col2 <= row2)
    first = lane < GLA_HEAD_DK
    zero = jnp.zeros((CHUNK, LANES), BF16)

    pre = _dot(z_ref[0, :, 0:2 * GATE_RANK].astype(BF16), wg) + bg
    g = (jnp.minimum(pre, 0.0) - jnp.log1p(jnp.exp(-jnp.abs(pre)))) * (1.0 / GATE_NORM)
    g_hi, g_lo = _split_bf16(g)
    b_parts, last_parts, decays = [], [], []
    for c in range(n_chunks):
        rows = slice(c * CHUNK, (c + 1) * CHUNK)
        b_c = _dot(tri, g_hi[rows]) + _dot(tri, g_lo[rows])
        b_last = b_c[0:1] if reverse else b_c[CHUNK - 1:CHUNK]
        b_parts.append(b_c)
        last_parts.append(jnp.broadcast_to(b_last, (CHUNK, GLA_DK)))
        decays.append(jnp.broadcast_to(jnp.exp(b_last), (LANES, GLA_DK)).T)
    b = jnp.concatenate(b_parts, axis=0)
    need_out = o_ref is not None
    k = k_ref[0].astype(F32)
    if need_out:
        q_e = (q_ref[0].astype(F32) * (GLA_HEAD_DK ** -0.5) * jnp.exp(b)).astype(BF16)
        k_e = k * jnp.exp(-b)
    k_t = (k * jnp.exp(jnp.concatenate(last_parts, axis=0) - b)).astype(BF16)
    v = v_ref[0].astype(BF16)

    o_intra, kvs = [], []
    for c in range(n_chunks):
        rows = slice(c * CHUNK, (c + 1) * CHUNK)
        o_c, kv_c = [], []
        for p in range(GLA_HEADS // 2):
            lanes = slice(p * LANES, (p + 1) * LANES)
            wide = slice(2 * p * LANES, 2 * (p + 1) * LANES)
            vp = v[rows, wide]
            if need_out:
                kg = k_e[rows, lanes]
                k_blk = jnp.concatenate([jnp.where(first, kg, 0.0), jnp.where(first, 0.0, kg)],
                                        axis=0).astype(BF16)
                scores = lax.dot_general(q_e[rows, lanes], k_blk, (((1,), (1,)), ((), ())),
                                         preferred_element_type=F32)
                a = jnp.where(keep_pair, scores, 0.0).astype(BF16)
                v_blk = jnp.concatenate(
                    [jnp.concatenate([vp[:, :LANES], zero], axis=1),
                     jnp.concatenate([zero, vp[:, LANES:]], axis=1)], axis=0)
                o_c.append((a, v_blk))
            kv_c.append(lax.dot_general(k_t[rows, lanes], vp,
                                        (((0,), (0,)), ((), ())), preferred_element_type=F32))
        o_intra.append(o_c)
        kvs.append(kv_c)

    state = [s_ref[h] for h in range(GLA_HEADS)]
    zero_s = jnp.zeros((GLA_HEAD_DK, GLA_HEAD_DV), F32)
    h0 = slice(0, GLA_HEAD_DK)
    h1 = slice(GLA_HEAD_DK, 2 * GLA_HEAD_DK)
    for c in (range(n_chunks - 1, -1, -1) if reverse else range(n_chunks)):
        rows = slice(c * CHUNK, (c + 1) * CHUNK)
        for p in range(GLA_HEADS // 2):
            lanes = slice(p * LANES, (p + 1) * LANES)
            wide = slice(2 * p * LANES, 2 * (p + 1) * LANES)
            if need_out:
                s_blk = jnp.concatenate(
                    [jnp.concatenate([state[2 * p], zero_s], axis=1),
                     jnp.concatenate([zero_s, state[2 * p + 1]], axis=1)], axis=0).astype(BF16)
                a, v_blk = o_intra[c][p]
                o_ref[0, rows, wide] = _dot(jnp.concatenate([q_e[rows, lanes], a], axis=1),
                                            jnp.concatenate([s_blk, v_blk], axis=0)).astype(o_ref.dtype)
            dp = decays[c][lanes]
            kv = kvs[c][p]
            state[2 * p] = dp[h0] * state[2 * p] + kv[h0, 0:LANES]
            state[2 * p + 1] = dp[h1] * state[2 * p + 1] + kv[h1, LANES:2 * LANES]
    for h in range(GLA_HEADS):
        s_ref[h] = state[h]


def _gla_kernel(*refs, need_out):
    if need_out:
        (qf_ref, kf_ref, vf_ref, zf_ref, qb_ref, kb_ref, vb_ref, zb_ref,
         wgf_ref, bgf_ref, wgb_ref, bgb_ref, sf0_ref, sb0_ref,
         of_ref, ob_ref, sf_out_ref, sb_out_ref, sf_ref, sb_ref) = refs
    else:
        (kf_ref, vf_ref, zf_ref, kb_ref, vb_ref, zb_ref,
         wgf_ref, bgf_ref, wgb_ref, bgb_ref, sf0_ref, sb0_ref,
         sf_out_ref, sb_out_ref, sf_ref, sb_ref) = refs
        qf_ref = qb_ref = of_ref = ob_ref = None
    i = pl.program_id(1)

    @pl.when(i == 0)
    def _():
        sf_ref[...] = sf0_ref[0]
        sb_ref[...] = sb0_ref[0]

    _gla_block(qf_ref, kf_ref, vf_ref, zf_ref, wgf_ref[...], bgf_ref[...], sf_ref, of_ref, False)
    _gla_block(qb_ref, kb_ref, vb_ref, zb_ref, wgb_ref[...], bgb_ref[...], sb_ref, ob_ref, True)

    @pl.when(i == pl.num_programs(1) - 1)
    def _():
        sf_out_ref[0] = sf_ref[...]
        sb_out_ref[0] = sb_ref[...]


def _gla(q, k, v, z, wgf, bgf, wgb, bgb, sf0, sb0):
    need_out = q is not None
    bsz, length, _ = k.shape
    tb = min(GLA_BLOCK, length)
    assert length % tb == 0 and tb % CHUNK == 0
    nb = length // tb

    def fwd(w):
        return pl.BlockSpec((1, tb, w), lambda b, i: (b, i, 0))

    def bwd(w):
        return pl.BlockSpec((1, tb, w), lambda b, i: (b, nb - 1 - i, 0))

    def const2(shape):
        return pl.BlockSpec(shape, lambda b, i: (0, 0))

    state = pl.BlockSpec((1, GLA_HEADS, GLA_HEAD_DK, GLA_HEAD_DV), lambda b, i: (b, 0, 0, 0))
    state_shape = jax.ShapeDtypeStruct((bsz, GLA_HEADS, GLA_HEAD_DK, GLA_HEAD_DV), F32)
    o_shape = jax.ShapeDtypeStruct((bsz, length, GLA_DV), F32)
    q_f = [fwd(GLA_DK)] if need_out else []
    q_b = [bwd(GLA_DK)] if need_out else []
    q_arg = [q] if need_out else []
    return pl.pallas_call(
        functools.partial(_gla_kernel, need_out=need_out),
        grid=(bsz, nb),
        in_specs=(q_f + [fwd(GLA_DK), fwd(GLA_DV), fwd(Z_PAD)]
                  + q_b + [bwd(GLA_DK), bwd(GLA_DV), bwd(Z_PAD)]
                  + [const2((2 * GATE_RANK, GLA_DK)), const2((1, GLA_DK)),
                     const2((2 * GATE_RANK, GLA_DK)), const2((1, GLA_DK)),
                     state, state]),
        out_specs=([fwd(GLA_DV), bwd(GLA_DV)] if need_out else []) + [state, state],
        out_shape=([o_shape, o_shape] if need_out else []) + [state_shape, state_shape],
        scratch_shapes=[pltpu.VMEM((GLA_HEADS, GLA_HEAD_DK, GLA_HEAD_DV), F32),
                        pltpu.VMEM((GLA_HEADS, GLA_HEAD_DK, GLA_HEAD_DV), F32)],
        compiler_params=pltpu.CompilerParams(dimension_semantics=("arbitrary", "arbitrary"),
                                             vmem_limit_bytes=VMEM_LIMIT),
        name="gla",
    )(*q_arg, k, v, z, *q_arg, k, v, z, wgf, bgf, wgb, bgb, sf0, sb0)


def _conv_kernel(grow_ref, gcol_ref, wrow_ref, wcol_ref, cb_ref, lg_ref, lb_ref, y_ref,
                 rowpad_ref, colpad_ref, yrow_ref):
    t = pl.program_id(1)
    n_rows = gcol_ref.shape[1]
    pad_rows = CONV_ROWS
    pad_w = 2 * SUBLANES

    @pl.when(t == 0)
    def _():
        rowpad_ref[...] = jnp.zeros(rowpad_ref.shape, F32)
        zeros = jnp.zeros((pad_rows, GRID_W, CONV_HALF), F32)
        colpad_ref[0:pad_rows] = zeros
        colpad_ref[pad_rows + n_rows:pad_rows + n_rows + pad_rows] = zeros

        def copy(j, carry):
            colpad_ref[pl.ds(pad_rows + j * CONV_ROWS, CONV_ROWS)] = (
                gcol_ref[0, pl.ds(j * CONV_ROWS, CONV_ROWS)].astype(F32))
            return carry
        lax.fori_loop(0, n_rows // CONV_ROWS, copy, 0)

    for lg in range(CONV_HALF // LANES):
        rowpad_ref[lg, :, pad_w:pad_w + GRID_W, :] = grow_ref[0, :, :, lg * LANES:(lg + 1) * LANES].astype(F32)

    half_w = GRID_W // 2

    def row_body(r, carry):
        pieces = []
        for lg in range(CONV_HALF // LANES):
            lanes = slice(lg * LANES, (lg + 1) * LANES)
            for parity in range(2):
                acc = jnp.broadcast_to(cb_ref[:, lanes], (half_w, LANES))
                for kk in range(CONV_K):
                    start = pad_w + kk - CONV_PAD + parity
                    acc = acc + (wrow_ref[kk:kk + 1, lanes]
                                 * rowpad_ref[lg, r, pl.ds(start, half_w, stride=2), :])
                yrow_ref[lg, pl.ds(parity, half_w, stride=2), :] = acc
            pieces.append(yrow_ref[lg])
        for lg in range(CONV_HALF // LANES):
            lanes = slice(lg * LANES, (lg + 1) * LANES)
            acc = jnp.broadcast_to(cb_ref[:, CONV_HALF + lg * LANES:CONV_HALF + (lg + 1) * LANES],
                                   (GRID_W, LANES))
            for kk in range(CONV_K):
                src = t * CONV_ROWS + r + (pad_rows + kk - CONV_PAD)
                acc = acc + wcol_ref[kk:kk + 1, lanes] * colpad_ref[src, :, lanes]
            pieces.append(acc)
        y = jnp.concatenate(pieces, axis=1)
        mu = jnp.mean(y, axis=-1, keepdims=True)
        var = jnp.mean(y * y, axis=-1, keepdims=True) - mu * mu
        yn = (y - mu) * lax.rsqrt(var + EPS) * lg_ref[...] + lb_ref[...]
        y_ref[0, r] = (yn * _sigmoid(yn)).astype(y_ref.dtype)
        return carry

    lax.fori_loop(0, CONV_ROWS, row_body, 0, unroll=2)


def _conv_module(glu4, w_row, w_col, conv_b, ln_g, ln_b):
    bsz, n_rows, gw, _ = glu4.shape
    return pl.pallas_call(
        _conv_kernel,
        grid=(bsz, n_rows // CONV_ROWS),
        in_specs=[pl.BlockSpec((1, CONV_ROWS, gw, CONV_HALF), lambda b, t: (b, t, 0, 0)),
                  pl.BlockSpec((1, n_rows, gw, CONV_HALF), lambda b, t: (b, 0, 0, 1)),
                  pl.BlockSpec((4 * SUBLANES, CONV_HALF), lambda b, t: (0, 0)),
                  pl.BlockSpec((4 * SUBLANES, CONV_HALF), lambda b, t: (0, 0)),
                  pl.BlockSpec((1, CONV_W), lambda b, t: (0, 0)),
                  pl.BlockSpec((1, CONV_W), lambda b, t: (0, 0)),
                  pl.BlockSpec((1, CONV_W), lambda b, t: (0, 0))],
        out_specs=pl.BlockSpec((1, CONV_ROWS, gw, CONV_W), lambda b, t: (b, t, 0, 0)),
        out_shape=jax.ShapeDtypeStruct((bsz, n_rows, gw, CONV_W), F32),
        scratch_shapes=[pltpu.VMEM((CONV_HALF // LANES, CONV_ROWS, gw + 4 * SUBLANES, LANES), F32),
                        pltpu.VMEM((n_rows + 2 * CONV_ROWS, gw, CONV_HALF), F32),
                        pltpu.VMEM((CONV_HALF // LANES, gw, LANES), F32)],
        compiler_params=pltpu.CompilerParams(dimension_semantics=("arbitrary", "arbitrary"),
                                             vmem_limit_bytes=VMEM_LIMIT),
        name="conv_module",
    )(glu4, glu4, w_row, w_col, conv_b, ln_g, ln_b)


def _ffn_kernel(x_p, x_m, x_n, yc_p, yc_m, yc_n, of_p, of_m, of_n, ob_p, ob_m, ob_n,
                      og_p, og_m, og_n, mod_ref, gn_ref, n2_ref, fg_ref, wout_ref, wup_ref,
                      dw_ref, dwb_ref, wdown_ref, out_ref, a_ref):
    i = pl.program_id(1)
    last = pl.num_programs(1) - 1
    tm = x_m.shape[1]
    main = slice(HALO, HALO + tm)

    def ext(p, m, n):
        return jnp.concatenate([p[0], m[0], n[0]], axis=0)

    mod = mod_ref[0]
    gate1 = mod[:, 2 * D_MODEL:3 * D_MODEL]
    shift2 = mod[:, 3 * D_MODEL:4 * D_MODEL]
    scale2 = mod[:, 4 * D_MODEL:5 * D_MODEL]
    gate2 = mod[:, 5 * D_MODEL:6 * D_MODEL]

    o = ext(of_p, of_m, of_n) + ext(ob_p, ob_m, ob_n)
    og = ext(og_p, og_m, og_n)
    parts = [ext(yc_p, yc_m, yc_n).astype(BF16)]
    for h in range(GLA_HEADS):
        lanes = slice(h * GLA_HEAD_DV, (h + 1) * GLA_HEAD_DV)
        oh = o[:, lanes]
        ms = jnp.mean(oh * oh, axis=-1, keepdims=True)
        ogh = og[:, lanes]
        parts.append((oh * lax.rsqrt(ms + EPS) * gn_ref[:, lanes] * (ogh * _sigmoid(ogh))).astype(BF16))
    mix = jnp.concatenate(parts, axis=1)
    x1 = ext(x_p, x_m, x_n) + gate1 * _dot(mix, wout_ref[...])

    ms = jnp.mean(x1 * x1, axis=-1, keepdims=True)
    h2_f32 = (x1 * lax.rsqrt(ms + EPS) * n2_ref[...]) * (1.0 + scale2) + shift2
    h2 = h2_f32.astype(BF16)
    h2_main = h2_f32[main].astype(BF16)

    row = lax.broadcasted_iota(jnp.int32, (tm, 1), 0)
    kill_prev = jnp.logical_and(row == 0, i == 0)
    kill_next = jnp.logical_and(row == tm - 1, i == last)
    n_chunks = FFN_HIDDEN // FFN_CHUNK

    def up(j):
        a_ref[j % UP_SLOTS] = _dot(h2, wup_ref[:, j * FFN_CHUNK:(j + 1) * FFN_CHUNK])
        return _dot(h2_main, wup_ref[:, FFN_HIDDEN + j * FFN_CHUNK:FFN_HIDDEN + (j + 1) * FFN_CHUNK])

    acc = None
    vals = {j: up(j) for j in range(UP_SLOTS - 1)}
    pending = None
    for j in range(n_chunks):
        if j + UP_SLOTS - 1 < n_chunks:
            vals[j + UP_SLOTS - 1] = up(j + UP_SLOTS - 1)
        if pending is not None:
            part = _dot(pending[0], wdown_ref[pending[1], :])
            acc = part if acc is None else acc + part
        val = vals.pop(j)
        cols = slice(j * FFN_CHUNK, (j + 1) * FFN_CHUNK)
        a_cur = a_ref.at[j % UP_SLOTS]
        a_prev = jnp.where(kill_prev, 0.0, a_cur[HALO - 1:HALO - 1 + tm, :])
        a_next = jnp.where(kill_next, 0.0, a_cur[HALO + 1:HALO + 1 + tm, :])
        ac = (dw_ref[0:1, cols] * a_prev + dw_ref[1:2, cols] * a_cur[main, :]
              + dw_ref[2:3, cols] * a_next + dwb_ref[:, cols])
        f = (ac * _sigmoid(ac) * val).astype(BF16)
        pending = (f, cols)
    acc = acc + _dot(pending[0], wdown_ref[pending[1], :])

    x2 = x1[main] + gate2 * acc
    ms = jnp.mean(x2 * x2, axis=-1, keepdims=True)
    out_ref[0] = x2 * lax.rsqrt(ms + EPS) * fg_ref[...]


def _ffn(x, y_conv, o_f, o_b, og, mod, gn, n2, fg, w_out, w_up, dw, dwb, w_down):
    bsz, length, d = x.shape
    tm = FFN_TILE
    def trio(w, halo=HALO):
        per = tm // halo
        n_halo = length // halo
        return [pl.BlockSpec((1, halo, w), lambda b, i: (b, jnp.maximum(i * per - 1, 0), 0)),
                pl.BlockSpec((1, tm, w), lambda b, i: (b, i, 0)),
                pl.BlockSpec((1, halo, w), lambda b, i: (b, jnp.minimum((i + 1) * per, n_halo - 1), 0))]

    def const2(shape):
        return pl.BlockSpec(shape, lambda b, i: (0, 0))

    in_specs = (trio(d) + trio(CONV_W) + trio(GLA_DV) + trio(GLA_DV) + trio(GLA_DV)
                + [pl.BlockSpec((1, 1, N_MOD * d), lambda b, i: (b, 0, 0)),
                   const2((1, GLA_DV)), const2((1, d)), const2((1, d)),
                   const2((d, d)), const2((d, 2 * FFN_HIDDEN)),
                   const2((SUBLANES, FFN_HIDDEN)), const2((1, FFN_HIDDEN)),
                   const2((FFN_HIDDEN, d))])
    return pl.pallas_call(
        _ffn_kernel,
        grid=(bsz, length // tm),
        in_specs=in_specs,
        out_specs=pl.BlockSpec((1, tm, d), lambda b, i: (b, i, 0)),
        out_shape=jax.ShapeDtypeStruct((bsz, length, d), F32),
        scratch_shapes=[pltpu.VMEM((UP_SLOTS, tm + 2 * HALO, FFN_CHUNK), F32)],
        compiler_params=pltpu.CompilerParams(dimension_semantics=("arbitrary", "arbitrary"),
                                             vmem_limit_bytes=VMEM_LIMIT),
        name="out_proj_ffn",
    )(x, x, x, y_conv, y_conv, y_conv, o_f, o_f, o_f, o_b, o_b, o_b, og, og, og,
      mod, gn, n2, fg, w_out, w_up, dw, dwb, w_down)


def kernel(x, c, ctx, c_ctx, w_mod, b_mod, norm1_g, w_in, conv_dw, conv_b, conv_ln_g, conv_ln_b,
           w_gf, b_gf, w_gb, b_gb, gla_norm_g, w_out, norm2_g, w_up, ffn_dw, ffn_dw_b, w_down, final_g):
    bsz, seq, d = x.shape
    ctx_len = ctx.shape[1]
    layer = 0

    cvec = jnp.concatenate([c, c_ctx[None, :], jnp.zeros((SUBLANES - bsz - 1, d), F32)], axis=0)
    mods = _modulation(cvec, w_mod[layer], b_mod[layer][None, :])
    mod_lat = mods[:bsz][:, None, :]
    mod_ctx = jnp.broadcast_to(mods[bsz][None, None, :], (bsz, 1, N_MOD * d))

    assert w_in.shape[2] == D_IN_PAD
    w_in_p = w_in[layer]
    zeros_g = jnp.zeros((GATE_RANK, GLA_DK), F32)
    wgf = jnp.concatenate([w_gf[layer], zeros_g], axis=0).astype(BF16)
    wgb = jnp.concatenate([zeros_g, w_gb[layer]], axis=0).astype(BF16)
    bgf = b_gf[layer][None, :]
    bgb = b_gb[layer][None, :]
    n1 = norm1_g[layer][None, :]

    k_c, v_c, z_c = _input_proj(ctx, mod_ctx, n1, w_in_p, ctx_len, full=False)
    zero_state = jnp.zeros((bsz, GLA_HEADS, GLA_HEAD_DK, GLA_HEAD_DV), F32)
    s_f, s_b = _gla(None, k_c, v_c, z_c, wgf, bgf, wgb, bgb, zero_state, zero_state)

    glu, q, k, v, og, z = _input_proj(x, mod_lat, n1, w_in_p, IN_TILE)
    o_f, o_b, _, _ = _gla(q, k, v, z, wgf, bgf, wgb, bgb, s_f, s_b)

    dw = jnp.pad(conv_dw[layer], ((0, 4 * SUBLANES - CONV_K), (0, 0)))
    y_conv = _conv_module(glu.reshape(bsz, seq // GRID_W, GRID_W, CONV_W),
                          dw[:, :CONV_HALF], dw[:, CONV_HALF:], conv_b[layer][None, :],
                          conv_ln_g[layer][None, :], conv_ln_b[layer][None, :])
    y_conv = y_conv.reshape(bsz, seq, CONV_W)

    ffn_w = jnp.pad(ffn_dw[layer], ((0, SUBLANES - ffn_dw.shape[1]), (0, 0)))
    return _ffn(x, y_conv, o_f, o_b, og, mod_lat, gla_norm_g[layer][None, :], norm2_g[layer][None, :],
                      final_g[None, :], w_out[layer].astype(BF16), w_up[layer].astype(BF16),
                      ffn_w, ffn_dw_b[layer][None, :], w_down[layer].astype(BF16))
```
